```python
import jax, jax.numpy as jnp
from jax import lax
import numpy as np

D_MODEL = 1024
BATCH = 8
SEQ = 4096
DEPTH = 4

CHUNK = 64
EPS = 1e-6
GLA_HEADS = 4
GLA_DK = (D_MODEL // 2) // GLA_HEADS
GLA_DV = D_MODEL // GLA_HEADS
GLA_KW = GLA_HEADS * GLA_DK
GLA_VW = GLA_HEADS * GLA_DV
GLA_RANK = 16
GLA_TAU = 16.0
LRU_WIDTH = D_MODEL
LRU_BLOCKS = 16
LRU_BW = LRU_WIDTH // LRU_BLOCKS
LRU_CONV = 4
LRU_C = 8.0
FFN_DFF = 3 * D_MODEL
FFN_CONV = 3

IN_SPLITS = (GLA_KW, GLA_KW, GLA_VW, GLA_VW, GLA_RANK, LRU_WIDTH, LRU_WIDTH, D_MODEL, D_MODEL)
IN_WIDTH = int(sum(IN_SPLITS))
IN_POINTS = tuple(int(p) for p in np.cumsum(IN_SPLITS)[:-1])

kernel_name = "hybrid_gla_rglru_convffn_trunk"


def rmsnorm(x, g):
    xf = x.astype(jnp.float32)
    y = xf * lax.rsqrt(jnp.mean(xf * xf, axis=-1, keepdims=True) + EPS)
    return (y * g.astype(jnp.float32)).astype(x.dtype)


def causal_dwconv(x, w, b):
    width = w.shape[0]
    T = x.shape[1]
    xp = jnp.pad(x, ((0, 0), (width - 1, 0), (0, 0)))
    y = b
    for j in range(width):
        y = y + xp[:, j:j + T] * w[j]
    return y


def gla_chunked(q, k, v, log_alpha):
    B, T, H, DK = q.shape
    DV = v.shape[-1]
    nc = T // CHUNK

    def to_chunks(t):
        return jnp.moveaxis(t.astype(jnp.float32).reshape(B, nc, CHUNK, H, t.shape[-1]), 1, 0)

    qc, kc, vc, ac = to_chunks(q), to_chunks(k), to_chunks(v), to_chunks(log_alpha)

    def step(S, inp):
        q_, k_, v_, a_ = inp
        cum = jnp.cumsum(a_, axis=1)
        tot = cum[:, -1]
        k_dec = k_ * jnp.exp(tot[:, None] - cum)
        S = jnp.exp(tot)[..., None] * S + jnp.einsum('bchk,bchv->bhkv', k_dec, v_)
        o = jnp.einsum('bchk,bhkv->bchv', q_, S)
        return S, o

    S0 = jnp.zeros((B, H, DK, DV), jnp.float32)
    _, o = lax.scan(step, S0, (qc, kc, vc, ac))
    return jnp.moveaxis(o, 0, 1).reshape(B, T, H, DV)


def rg_lru(x, w_a, b_a, w_x, b_x, lam):
    B, T, W = x.shape
    xf = x.astype(jnp.float32)
    xb = xf.reshape(B, T, LRU_BLOCKS, LRU_BW)
    r = jax.nn.sigmoid(jnp.einsum('btni,nij->btnj', xb, w_a.astype(jnp.float32)).reshape(B, T, W) + b_a)
    i = jax.nn.sigmoid(jnp.einsum('btni,nij->btnj', xb, w_x.astype(jnp.float32)).reshape(B, T, W) + b_x)
    log_a = LRU_C * r * jax.nn.log_sigmoid(lam.astype(jnp.float32))
    a = jnp.exp(log_a)
    u = jnp.sqrt(-jnp.expm1(2.0 * log_a)) * (i * xf)

    def combine(e1, e2):
        a1, b1 = e1
        a2, b2 = e2
        return a1 * a2, a2 * b1 + b2

    _, h = lax.associative_scan(combine, (a, u), axis=1)
    return h.astype(x.dtype)


def heads(t, n):
    B, T, C = t.shape
    return t.reshape(B, T, n, C // n)


def setup_inputs(seed: int = 0) -> dict:
    key = jax.random.key(seed)
    ks = jax.random.split(key, 24)
    f32 = jnp.float32

    def nrm(k, shape, fan_in):
        return jax.random.normal(k, shape, f32) * (fan_in ** -0.5)

    def gain(k, shape):
        return 1.0 + 0.02 * jax.random.normal(k, shape, f32)

    def bias(k, shape):
        return 0.01 * jax.random.normal(k, shape, f32)

    a0 = jax.random.uniform(ks[14], (DEPTH, LRU_WIDTH), f32, minval=0.9, maxval=0.999)
    s = a0 ** (1.0 / LRU_C)
    lam = jnp.log(s) - jnp.log1p(-s)

    return {
        "x": jax.random.normal(ks[0], (BATCH, SEQ, D_MODEL), f32),
        "norm_mix": gain(ks[1], (DEPTH, D_MODEL)),
        "w_in": nrm(ks[2], (DEPTH, D_MODEL, IN_WIDTH), D_MODEL),
        "w_alpha": nrm(ks[3], (DEPTH, GLA_RANK, GLA_KW), GLA_RANK),
        "b_alpha": bias(ks[4], (DEPTH, GLA_KW)),
        "gla_norm": gain(ks[5], (DEPTH, GLA_DV)),
        "w_out_gla": nrm(ks[6], (DEPTH, GLA_VW, D_MODEL), GLA_VW),
        "lru_conv_w": nrm(ks[7], (DEPTH, LRU_CONV, LRU_WIDTH), LRU_CONV),
        "lru_conv_b": bias(ks[8], (DEPTH, LRU_WIDTH)),
        "lru_w_a": nrm(ks[9], (DEPTH, LRU_BLOCKS, LRU_BW, LRU_BW), LRU_BW),
        "lru_b_a": bias(ks[10], (DEPTH, LRU_WIDTH)),
        "lru_w_x": nrm(ks[11], (DEPTH, LRU_BLOCKS, LRU_BW, LRU_BW), LRU_BW),
        "lru_b_x": bias(ks[12], (DEPTH, LRU_WIDTH)),
        "lru_lambda": lam,
        "w_out_lru": nrm(ks[13], (DEPTH, LRU_WIDTH, D_MODEL), LRU_WIDTH),
        "w_o": nrm(ks[15], (DEPTH, D_MODEL, D_MODEL), D_MODEL),
        "norm_ffn": gain(ks[16], (DEPTH, D_MODEL)),
        "w_up": nrm(ks[17], (DEPTH, D_MODEL, 2 * FFN_DFF), D_MODEL),
        "ffn_conv_w": nrm(ks[18], (DEPTH, FFN_CONV, 2 * FFN_DFF), FFN_CONV),
        "ffn_conv_b": bias(ks[19], (DEPTH, 2 * FFN_DFF)),
        "w_down": nrm(ks[20], (DEPTH, FFN_DFF, D_MODEL), FFN_DFF),
        "norm_final": gain(ks[21], (D_MODEL,)),
    }


def reference(x, norm_mix, w_in, w_alpha, b_alpha, gla_norm, w_out_gla,
              lru_conv_w, lru_conv_b, lru_w_a, lru_b_a, lru_w_x, lru_b_x, lru_lambda,
              w_out_lru, w_o, norm_ffn, w_up, ffn_conv_w, ffn_conv_b, w_down, norm_final):
    B, T, _ = x.shape
    for l in range(DEPTH):
        h = rmsnorm(x, norm_mix[l])
        proj = h @ w_in[l]
        q, k, v, g_out, a_code, xr, gr, gate_a, gate_b = jnp.split(proj, IN_POINTS, axis=-1)

        log_alpha = jax.nn.log_sigmoid(a_code @ w_alpha[l] + b_alpha[l]) / GLA_TAU
        o = gla_chunked(heads(q * (GLA_DK ** -0.5), GLA_HEADS), heads(k, GLA_HEADS),
                        heads(v, GLA_HEADS), heads(log_alpha, GLA_HEADS))
        o = o * lax.rsqrt(jnp.mean(o * o, axis=-1, keepdims=True) + EPS) * gla_norm[l].astype(jnp.float32)
        o = o.reshape(B, T, GLA_VW).astype(x.dtype)
        y_a = (o * jax.nn.silu(g_out)) @ w_out_gla[l]

        xc = causal_dwconv(xr, lru_conv_w[l], lru_conv_b[l])
        hr = rg_lru(xc, lru_w_a[l], lru_b_a[l], lru_w_x[l], lru_b_x[l], lru_lambda[l])
        y_b = (hr * jax.nn.gelu(gr)) @ w_out_lru[l]

        merged = jax.nn.sigmoid(gate_a) * y_a + jax.nn.sigmoid(gate_b) * y_b
        x = x + merged @ w_o[l]

        h = rmsnorm(x, norm_ffn[l])
        u = causal_dwconv(h @ w_up[l], ffn_conv_w[l], ffn_conv_b[l])
        val, gate = jnp.split(u, 2, axis=-1)
        x = x + (jax.nn.gelu(gate) * val) @ w_down[l]
    return rmsnorm(x, norm_final)
```

```python
import functools
import math

import jax
import jax.numpy as jnp
from jax import lax
from jax.experimental import pallas as pl
from jax.experimental.pallas import tpu as pltpu

D_MODEL = 1024
BATCH = 8
CHUNK = 64
ROWS = BATCH * CHUNK
EPS = 1e-6
GLA_HEADS = 4
GLA_DK = 128
GLA_DV = 256
GLA_KW = GLA_HEADS * GLA_DK
GLA_VW = GLA_HEADS * GLA_DV
GLA_RANK = 16
GLA_TAU = 16.0
LRU_BLOCKS = 16
LRU_BW = 64
LRU_CONV = 4
LRU_C = 8.0
FFN_DFF = 3 * D_MODEL
FFN_CONV = 3
FFN_TILE = 512

LANES = 128
SUBLANES = 8
N_SLABS = D_MODEL // LANES
VMEM_LIMIT_BYTES = 58 * 1024 * 1024

IN_SPLITS = (GLA_KW, GLA_KW, GLA_VW, GLA_VW, GLA_RANK, D_MODEL, D_MODEL, D_MODEL, D_MODEL)

assert BATCH == SUBLANES


def _bdot(a, b):
    return jnp.dot(a.astype(jnp.bfloat16), b.astype(jnp.bfloat16),
                   preferred_element_type=jnp.float32)


def _batch_rows(b):
    return pl.ds(b, CHUNK, stride=BATCH)


def _load_time_major(x_ref, slab_ref):
    for b in range(BATCH):
        xb = x_ref[b]
        for s in range(N_SLABS):
            slab_ref[s, _batch_rows(b), :] = xb[:, s * LANES:(s + 1) * LANES]
    return jnp.concatenate([slab_ref[s] for s in range(N_SLABS)], axis=1)


def _store_batch_major(val, slab_ref, out_ref):
    for s in range(N_SLABS):
        slab_ref[s] = val[:, s * LANES:(s + 1) * LANES]
    for b in range(BATCH):
        out_ref[b] = jnp.concatenate(
            [slab_ref[s, _batch_rows(b), :] for s in range(N_SLABS)], axis=1)


def _rmsnorm(x, g):
    return x * lax.rsqrt(jnp.mean(x * x, axis=-1, keepdims=True) + EPS) * g


def _log_sigmoid(z):
    return jnp.minimum(z, 0.0) - jnp.log1p(jnp.exp(-jnp.abs(z)))


def _causal_conv(xpad, w_ref, b_ref, width):
    y = b_ref[...]
    for j in range(width):
        y = y + xpad[j * BATCH:j * BATCH + ROWS] * w_ref[j:j + 1, :]
    return y


def _mixer_kernel(x_ref, g_ref, wq_ref, wk_ref, wv_ref, wg_ref, wac_ref, walpha_ref,
                  balpha_ref, gnorm_ref, wouta_ref, wxr_ref, wgr_ref, cw_ref, cb_ref,
                  wlru_ref, ba_ref, bx_ref, lam_ref, woutb_ref, wga_ref, wgb_ref,
                  wo_ref, out_ref,
                  xs_ref, q_ref, k_ref, la_ref, v_ref, o_ref, st_ref, tail_ref, h_ref):
    step = pl.program_id(0)

    @pl.when(step == 0)
    def _():
        st_ref[...] = jnp.zeros_like(st_ref)
        tail_ref[...] = jnp.zeros_like(tail_ref)
        h_ref[...] = jnp.zeros_like(h_ref)

    x = _load_time_major(x_ref, xs_ref)
    hn = _rmsnorm(x, g_ref[...]).astype(jnp.bfloat16)

    q = _bdot(hn, wq_ref[...]) * (GLA_DK ** -0.5)
    k = _bdot(hn, wk_ref[...])
    code = _bdot(hn, wac_ref[...])
    la = _log_sigmoid(_bdot(code, walpha_ref[...]) + balpha_ref[...]) / GLA_TAU
    for h in range(GLA_HEADS):
        q_ref[h] = q[:, h * GLA_DK:(h + 1) * GLA_DK]
        k_ref[h] = k[:, h * GLA_DK:(h + 1) * GLA_DK]
        la_ref[h] = la[:, h * GLA_DK:(h + 1) * GLA_DK]
    v = _bdot(hn, wv_ref[...])
    for s in range(N_SLABS):
        v_ref[s] = v[:, s * LANES:(s + 1) * LANES]

    row = lax.broadcasted_iota(jnp.int32, (CHUNK, CHUNK), 0)
    col = lax.broadcasted_iota(jnp.int32, (CHUNK, CHUNK), 1)
    tri = (col <= row).astype(jnp.bfloat16)

    def gla_batch(b, carry):
        rows = _batch_rows(b)
        for h in range(GLA_HEADS):
            qb = q_ref[h, rows, :]
            kb = k_ref[h, rows, :]
            lab = la_ref[h, rows, :]
            vb = jnp.concatenate([v_ref[2 * h, rows, :], v_ref[2 * h + 1, rows, :]], axis=1)
            la_hi = lab.astype(jnp.bfloat16)
            la_lo = lab - la_hi.astype(jnp.float32)
            cum = _bdot(tri, la_hi) + _bdot(tri, la_lo)
            tot = cum[CHUNK - 1:CHUNK, :]
            kdec = kb * jnp.exp(tot - cum)
            st = st_ref[b * GLA_HEADS + h] * jnp.exp(tot) + lax.dot_general(
                vb.astype(jnp.bfloat16), kdec.astype(jnp.bfloat16),
                (((0,), (0,)), ((), ())), preferred_element_type=jnp.float32)
            st_ref[b * GLA_HEADS + h] = st
            o = lax.dot_general(qb.astype(jnp.bfloat16), st.astype(jnp.bfloat16),
                                (((1,), (1,)), ((), ())),
                                preferred_element_type=jnp.float32)
            o = o * lax.rsqrt(jnp.mean(o * o, axis=-1, keepdims=True) + EPS)
            o_ref[2 * h, rows, :] = o[:, :LANES]
            o_ref[2 * h + 1, rows, :] = o[:, LANES:]
        return carry

    lax.fori_loop(0, BATCH, gla_batch, 0)

    o = jnp.concatenate([o_ref[s] for s in range(N_SLABS)], axis=1)
    g_out = _bdot(hn, wg_ref[...])
    y_a = _bdot(o * gnorm_ref[...] * jax.nn.silu(g_out), wouta_ref[...])

    xr = _bdot(hn, wxr_ref[...])
    xpad = jnp.concatenate([tail_ref[...], xr], axis=0)
    tail_ref[...] = xr[ROWS - (LRU_CONV - 1) * BATCH:]
    xc = _causal_conv(xpad, cw_ref, cb_ref, LRU_CONV)
    log_lam = LRU_C * _log_sigmoid(lam_ref[...])
    hr_tiles = []
    for s in range(N_SLABS):
        lanes = slice(s * LANES, (s + 1) * LANES)
        xs = xc[:, lanes]
        gates = _bdot(xs, wlru_ref[s])
        r = jax.nn.sigmoid(gates[:, :LANES] + ba_ref[:, lanes])
        i = jax.nn.sigmoid(gates[:, LANES:] + bx_ref[:, lanes])
        log_a = r * log_lam[:, lanes]
        a = jnp.exp(log_a)
        u = jnp.sqrt(-jnp.tanh(log_a) * (a * a + 1.0)) * (i * xs)
        hcur = h_ref[:, lanes]
        hs = []
        for t in range(CHUNK):
            rws = slice(t * BATCH, (t + 1) * BATCH)
            hcur = a[rws] * hcur + u[rws]
            hs.append(hcur)
        h_ref[:, lanes] = hcur
        hr_tiles.append(jnp.concatenate(hs, axis=0))
    hr = jnp.concatenate(hr_tiles, axis=1)
    gr = _bdot(hn, wgr_ref[...])
    y_b = _bdot(hr * jax.nn.gelu(gr), woutb_ref[...])

    gate_a = _bdot(hn, wga_ref[...])
    gate_b = _bdot(hn, wgb_ref[...])
    merged = jax.nn.sigmoid(gate_a) * y_a + jax.nn.sigmoid(gate_b) * y_b
    _store_batch_major(x + _bdot(merged, wo_ref[...]), xs_ref, out_ref)


def _ffn_kernel(x_ref, g_ref, wup_ref, cw_ref, cb_ref, wdown_ref, gfin_ref, out_ref,
                xs_ref, tail_ref, *, final_norm):
    step = pl.program_id(0)

    @pl.when(step == 0)
    def _():
        tail_ref[...] = jnp.zeros_like(tail_ref)

    x = _load_time_major(x_ref, xs_ref)
    hn = _rmsnorm(x, g_ref[...]).astype(jnp.bfloat16)
    ctx = (FFN_CONV - 1) * BATCH
    acc = x
    for j in range(FFN_DFF // FFN_TILE):
        act = []
        for base in (j * FFN_TILE, FFN_DFF + j * FFN_TILE):
            cols = slice(base, base + FFN_TILE)
            up = _bdot(hn, wup_ref[:, cols])
            upad = jnp.concatenate([tail_ref[:, cols], up], axis=0)
            tail_ref[:, cols] = up[ROWS - ctx:]
            act.append(_causal_conv(upad, cw_ref.at[:, cols], cb_ref.at[:, cols], FFN_CONV))
        val, gate = act
        acc = acc + _bdot(jax.nn.gelu(gate) * val,
                          wdown_ref[j * FFN_TILE:(j + 1) * FFN_TILE, :])
    if final_norm:
        acc = _rmsnorm(acc, gfin_ref[...])
    _store_batch_major(acc, xs_ref, out_ref)


def _resident(shape):
    zeros = (0,) * len(shape)
    return pl.BlockSpec(shape, lambda i: zeros, pipeline_mode=pl.Buffered(1))


def _x_spec():
    return pl.BlockSpec((BATCH, CHUNK, D_MODEL), lambda i: (0, i, 0))


def _compiler_params():
    return pltpu.CompilerParams(dimension_semantics=("arbitrary",),
                                vmem_limit_bytes=VMEM_LIMIT_BYTES)


def _mixer_call(x, params):
    seq = x.shape[1]
    slab = pltpu.VMEM((N_SLABS, ROWS, LANES), jnp.float32)
    head_slab = pltpu.VMEM((GLA_HEADS, ROWS, LANES), jnp.float32)
    return pl.pallas_call(
        _mixer_kernel,
        grid=(seq // CHUNK,),
        in_specs=[_x_spec()] + [_resident(p.shape) for p in params],
        out_specs=_x_spec(),
        out_shape=jax.ShapeDtypeStruct(x.shape, x.dtype),
        scratch_shapes=[
            slab,
            head_slab, head_slab, head_slab,
            slab, slab,
            pltpu.VMEM((BATCH * GLA_HEADS, GLA_DV, GLA_DK), jnp.float32),
            pltpu.VMEM(((LRU_CONV - 1) * BATCH, D_MODEL), jnp.float32),
            pltpu.VMEM((BATCH, D_MODEL), jnp.float32),
        ],
        compiler_params=_compiler_params(),
        name="token_mixer",
    )(x, *params)


def _ffn_call(x, params, final_norm):
    seq = x.shape[1]
    return pl.pallas_call(
        functools.partial(_ffn_kernel, final_norm=final_norm),
        grid=(seq // CHUNK,),
        in_specs=[_x_spec()] + [_resident(p.shape) for p in params],
        out_specs=_x_spec(),
        out_shape=jax.ShapeDtypeStruct(x.shape, x.dtype),
        scratch_shapes=[
            pltpu.VMEM((N_SLABS, ROWS, LANES), jnp.float32),
            pltpu.VMEM(((FFN_CONV - 1) * BATCH, 2 * FFN_DFF), jnp.float32),
        ],
        compiler_params=_compiler_params(),
        name="channel_mixer",
    )(x, *params)


def _lru_gate_weights(w_a, w_x):
    def blockdiag(w):
        w = w.reshape(N_SLABS, 2, LRU_BW, LRU_BW)
        z = jnp.zeros((N_SLABS, LRU_BW, LRU_BW), w.dtype)
        top = jnp.concatenate([w[:, 0], z], axis=2)
        bot = jnp.concatenate([z, w[:, 1]], axis=2)
        return jnp.concatenate([top, bot], axis=1)
    return jnp.concatenate([blockdiag(w_a), blockdiag(w_x)], axis=2).astype(jnp.bfloat16)


def kernel(x, norm_mix, w_in, w_alpha, b_alpha, gla_norm, w_out_gla, lru_conv_w, lru_conv_b,
           lru_w_a, lru_b_a, lru_w_x, lru_b_x, lru_lambda, w_out_lru, w_o, norm_ffn, w_up,
           ffn_conv_w, ffn_conv_b, w_down, norm_final):
    depth = w_in.shape[0]
    bf = jnp.bfloat16
    row = lambda a: a.reshape(1, -1)
    points = [0]
    for width in IN_SPLITS:
        points.append(points[-1] + width)
    for l in range(depth):
        wq, wk, wv, wg, wac, wxr, wgr, wga, wgb = (
            w_in[l, :, points[i]:points[i + 1]].astype(bf) for i in range(len(IN_SPLITS)))
        wac = jnp.pad(wac, ((0, 0), (0, LANES - GLA_RANK)))
        walpha = jnp.pad(w_alpha[l].astype(bf), ((0, LANES - GLA_RANK), (0, 0)))
        mixer_params = (
            row(norm_mix[l]), wq, wk, wv, wg, wac, walpha, row(b_alpha[l]),
            row(jnp.tile(gla_norm[l], GLA_HEADS)), w_out_gla[l].astype(bf),
            wxr, wgr, lru_conv_w[l], row(lru_conv_b[l]),
            _lru_gate_weights(lru_w_a[l], lru_w_x[l]), row(lru_b_a[l]), row(lru_b_x[l]),
            row(lru_lambda[l]), w_out_lru[l].astype(bf), wga, wgb, w_o[l].astype(bf))
        x = _mixer_call(x, mixer_params)
        ffn_params = (row(norm_ffn[l]), w_up[l].astype(bf), ffn_conv_w[l], row(ffn_conv_b[l]),
                      w_down[l].astype(bf), row(norm_final))
        x = _ffn_call(x, ffn_params, final_norm=(l == depth - 1))
    return x
```

```python
import functools

import jax
import jax.numpy as jnp
from jax import lax
from jax.experimental import pallas as pl
from jax.experimental.pallas import tpu as pltpu

D_MODEL = 1024
BATCH = 8
CHUNK = 64
ROWS = BATCH * CHUNK
EPS = 1e-6
GLA_HEADS = 4
GLA_DK = 128
GLA_DV = 256
GLA_KW = GLA_HEADS * GLA_DK
GLA_VW = GLA_HEADS * GLA_DV
GLA_RANK = 16
GLA_TAU = 16.0
LRU_BLOCKS = 16
LRU_BW = 64
LRU_CONV = 4
LRU_C = 8.0
FFN_DFF = 3 * D_MODEL
FFN_CONV = 3
FFN_TILE = 512

LANES = 128
SUBLANES = 8
N_SLABS = D_MODEL // LANES
VMEM_LIMIT_BYTES = 58 * 1024 * 1024

IN_SPLITS = (GLA_KW, GLA_KW, GLA_VW, GLA_VW, GLA_RANK, D_MODEL, D_MODEL, D_MODEL, D_MODEL)

assert BATCH == SUBLANES


def _bdot(a, b):
    return jnp.dot(a.astype(jnp.bfloat16), b.astype(jnp.bfloat16),
                   preferred_element_type=jnp.float32)


def _batch_rows(b):
    return pl.ds(b, CHUNK, stride=BATCH)


def _load_time_major(x_ref, slab_ref):
    for b in range(BATCH):
        xb = x_ref[b]
        for s in range(N_SLABS):
            slab_ref[s, _batch_rows(b), :] = xb[:, s * LANES:(s + 1) * LANES]
    return jnp.concatenate([slab_ref[s] for s in range(N_SLABS)], axis=1)


def _store_batch_major(val, slab_ref, out_ref):
    for s in range(N_SLABS):
        slab_ref[s] = val[:, s * LANES:(s + 1) * LANES]
    for b in range(BATCH):
        out_ref[b] = jnp.concatenate(
            [slab_ref[s, _batch_rows(b), :] for s in range(N_SLABS)], axis=1)


def _store_group_batch_major(val, c, slab_ref, out_ref):
    cols = slice(2 * c * LANES, (2 * c + 2) * LANES)
    slab_ref[2 * c] = val[:, :LANES]
    slab_ref[2 * c + 1] = val[:, LANES:]
    for b in range(BATCH):
        rows = _batch_rows(b)
        out_ref[b, :, cols] = jnp.concatenate(
            [slab_ref[2 * c, rows, :], slab_ref[2 * c + 1, rows, :]], axis=1)


def _rmsnorm(x, g):
    return x * lax.rsqrt(jnp.mean(x * x, axis=-1, keepdims=True) + EPS) * g


def _log_sigmoid(z):
    return jnp.minimum(z, 0.0) - jnp.log1p(jnp.exp(-jnp.abs(z)))


def _causal_conv(xpad, w_ref, b_ref, width):
    y = b_ref[...]
    for j in range(width):
        y = y + xpad[j * BATCH:j * BATCH + ROWS] * w_ref[j:j + 1, :]
    return y


def _mixer_kernel(x_ref, g_ref, wq_ref, wk_ref, wv_ref, wg_ref, wac_ref, walpha_ref,
                  balpha_ref, gnorm_ref, wouta_ref, wxr_ref, wgr_ref, cw_ref, cb_ref,
                  wlru_ref, ba_ref, bx_ref, lam_ref, woutb_ref, wga_ref, wgb_ref,
                  wo_ref, out_ref,
                  xs_ref, hn_ref, q_ref, kd_ref, etot_ref, v_ref, o_ref, st_ref, tail_ref,
                  h_ref):
    step = pl.program_id(0)

    @pl.when(step == 0)
    def _():
        st_ref[...] = jnp.zeros_like(st_ref)
        tail_ref[...] = jnp.zeros_like(tail_ref)
        h_ref[...] = jnp.zeros_like(h_ref)

    x = _load_time_major(x_ref, xs_ref)
    hn_ref[...] = _rmsnorm(x, g_ref[...]).astype(jnp.bfloat16)
    hn = hn_ref[...]

    xr = _bdot(hn, wxr_ref[...])
    code = _bdot(hn, wac_ref[...])
    k = _bdot(hn, wk_ref[...])
    la = _log_sigmoid(_bdot(code, walpha_ref[...]) + balpha_ref[...]) / GLA_TAU

    xpad = jnp.concatenate([tail_ref[...], xr], axis=0)
    tail_ref[...] = xr[ROWS - (LRU_CONV - 1) * BATCH:]
    xc = _causal_conv(xpad, cw_ref, cb_ref, LRU_CONV)
    gates = [_bdot(xc[:, s * LANES:(s + 1) * LANES], wlru_ref[s]) for s in range(N_SLABS)]

    run = jnp.zeros((BATCH, GLA_KW), jnp.float32)
    cums = []
    for t in range(CHUNK):
        run = run + la[t * BATCH:(t + 1) * BATCH]
        cums.append(run)
    etot_ref[...] = jnp.exp(run)
    kdec = k * jnp.exp(jnp.concatenate([run - c for c in cums], axis=0))
    q = _bdot(hn, wq_ref[...]) * (GLA_DK ** -0.5)
    for h in range(GLA_HEADS):
        q_ref[h] = q[:, h * GLA_DK:(h + 1) * GLA_DK]
        kd_ref[h] = kdec[:, h * GLA_DK:(h + 1) * GLA_DK]
    v = _bdot(hn, wv_ref[...])
    for s in range(N_SLABS):
        v_ref[s] = v[:, s * LANES:(s + 1) * LANES]
    g_out = _bdot(hn, wg_ref[...])

    for b in range(BATCH):
        rows = _batch_rows(b)
        for h in range(GLA_HEADS):
            kdb = kd_ref[h, rows, :]
            vb = jnp.concatenate([v_ref[2 * h, rows, :], v_ref[2 * h + 1, rows, :]], axis=1)
            decay = etot_ref[b:b + 1, h * GLA_DK:(h + 1) * GLA_DK]
            st_ref[b * GLA_HEADS + h] = st_ref[b * GLA_HEADS + h] * decay + lax.dot_general(
                vb.astype(jnp.bfloat16), kdb.astype(jnp.bfloat16),
                (((0,), (0,)), ((), ())), preferred_element_type=jnp.float32)
    gr = _bdot(hn, wgr_ref[...])
    for b in range(BATCH):
        rows = _batch_rows(b)
        for h in range(GLA_HEADS):
            o = lax.dot_general(q_ref[h, rows, :].astype(jnp.bfloat16),
                                st_ref[b * GLA_HEADS + h].astype(jnp.bfloat16),
                                (((1,), (1,)), ((), ())),
                                preferred_element_type=jnp.float32)
            o_ref[2 * h, rows, :] = o[:, :LANES]
            o_ref[2 * h + 1, rows, :] = o[:, LANES:]

    log_lam = LRU_C * _log_sigmoid(lam_ref[...])
    hr_tiles = []
    for s in range(N_SLABS):
        lanes = slice(s * LANES, (s + 1) * LANES)
        r = jax.nn.sigmoid(gates[s][:, :LANES] + ba_ref[:, lanes])
        i = jax.nn.sigmoid(gates[s][:, LANES:] + bx_ref[:, lanes])
        log_a = r * log_lam[:, lanes]
        a = jnp.exp(log_a)
        u = jnp.sqrt(-jnp.tanh(log_a) * (a * a + 1.0)) * (i * xc[:, lanes])
        hcur = h_ref[:, lanes]
        hs = []
        for t in range(CHUNK):
            rws = slice(t * BATCH, (t + 1) * BATCH)
            hcur = a[rws] * hcur + u[rws]
            hs.append(hcur)
        h_ref[:, lanes] = hcur
        hr_tiles.append(jnp.concatenate(hs, axis=0))
    hr = jnp.concatenate(hr_tiles, axis=1)

    gate_a = _bdot(hn, wga_ref[...])
    gate_b = _bdot(hn, wgb_ref[...])

    o_heads = []
    for h in range(GLA_HEADS):
        oh = jnp.concatenate([o_ref[2 * h], o_ref[2 * h + 1]], axis=1)
        o_heads.append(oh * lax.rsqrt(jnp.mean(oh * oh, axis=-1, keepdims=True) + EPS))
    o = jnp.concatenate(o_heads, axis=1)
    y_a = _bdot(o * gnorm_ref[...] * jax.nn.silu(g_out), wouta_ref[...])
    y_b = _bdot(hr * jax.nn.gelu(gr), woutb_ref[...])

    merged = (jax.nn.sigmoid(gate_a) * y_a + jax.nn.sigmoid(gate_b) * y_b).astype(jnp.bfloat16)
    for c in range(N_SLABS // 2):
        cols = slice(2 * c * LANES, (2 * c + 2) * LANES)
        res = jnp.concatenate([xs_ref[2 * c], xs_ref[2 * c + 1]], axis=1)
        res = res + _bdot(merged, wo_ref[:, cols])
        _store_group_batch_major(res, c, xs_ref, out_ref)


def _ffn_kernel(x_ref, g_ref, wup_ref, cw_ref, cb_ref, wdown_ref, gfin_ref, out_ref,
                xs_ref, hn_ref, tail_ref, *, final_norm):
    step = pl.program_id(0)

    @pl.when(step == 0)
    def _():
        tail_ref[...] = jnp.zeros_like(tail_ref)

    x = _load_time_major(x_ref, xs_ref)
    hn_ref[...] = _rmsnorm(x, g_ref[...]).astype(jnp.bfloat16)
    hn = hn_ref[...]
    ctx = (FFN_CONV - 1) * BATCH
    n_tiles = FFN_DFF // FFN_TILE

    def up_tile(j):
        act = []
        for base in (j * FFN_TILE, FFN_DFF + j * FFN_TILE):
            cols = slice(base, base + FFN_TILE)
            up = _bdot(hn, wup_ref[:, cols])
            upad = jnp.concatenate([tail_ref[:, cols], up], axis=0)
            tail_ref[:, cols] = up[ROWS - ctx:]
            act.append(_causal_conv(upad, cw_ref.at[:, cols], cb_ref.at[:, cols], FFN_CONV))
        val, gate = act
        return (jax.nn.gelu(gate) * val).astype(jnp.bfloat16)

    def down_rows(j):
        return slice(j * FFN_TILE, (j + 1) * FFN_TILE)

    acc = None
    act = up_tile(0)
    for j in range(1, n_tiles):
        act_next = up_tile(j)
        part = _bdot(act, wdown_ref[down_rows(j - 1), :])
        acc = part if acc is None else acc + part
        act = act_next
    last = down_rows(n_tiles - 1)
    if final_norm:
        y = jnp.concatenate([xs_ref[s] for s in range(N_SLABS)], axis=1)
        y = y + acc + _bdot(act, wdown_ref[last, :])
        _store_batch_major(_rmsnorm(y, gfin_ref[...]), xs_ref, out_ref)
    else:
        for c in range(N_SLABS // 2):
            cols = slice(2 * c * LANES, (2 * c + 2) * LANES)
            res = jnp.concatenate([xs_ref[2 * c], xs_ref[2 * c + 1]], axis=1)
            res = res + acc[:, cols] + _bdot(act, wdown_ref[last, cols])
            _store_group_batch_major(res, c, xs_ref, out_ref)


def _resident(shape):
    zeros = (0,) * len(shape)
    return pl.BlockSpec(shape, lambda i: zeros, pipeline_mode=pl.Buffered(1))


def _x_spec():
    return pl.BlockSpec((BATCH, CHUNK, D_MODEL), lambda i: (0, i, 0))


def _compiler_params():
    return pltpu.CompilerParams(dimension_semantics=("arbitrary",),
                                vmem_limit_bytes=VMEM_LIMIT_BYTES)


def _mixer_call(x, params):
    seq = x.shape[1]
    slab = pltpu.VMEM((N_SLABS, ROWS, LANES), jnp.float32)
    head_slab = pltpu.VMEM((GLA_HEADS, ROWS, LANES), jnp.float32)
    return pl.pallas_call(
        _mixer_kernel,
        grid=(seq // CHUNK,),
        in_specs=[_x_spec()] + [_resident(p.shape) for p in params],
        out_specs=_x_spec(),
        out_shape=jax.ShapeDtypeStruct(x.shape, x.dtype),
        scratch_shapes=[
            slab,
            pltpu.VMEM((ROWS, D_MODEL), jnp.bfloat16),
            head_slab, head_slab,
            pltpu.VMEM((BATCH, GLA_KW), jnp.float32),
            slab, slab,
            pltpu.VMEM((BATCH * GLA_HEADS, GLA_DV, GLA_DK), jnp.float32),
            pltpu.VMEM(((LRU_CONV - 1) * BATCH, D_MODEL), jnp.float32),
            pltpu.VMEM((BATCH, D_MODEL), jnp.float32),
        ],
        compiler_params=_compiler_params(),
        name="token_mixer",
    )(x, *params)


def _ffn_call(x, params, final_norm):
    seq = x.shape[1]
    return pl.pallas_call(
        functools.partial(_ffn_kernel, final_norm=final_norm),
        grid=(seq // CHUNK,),
        in_specs=[_x_spec()] + [_resident(p.shape) for p in params],
        out_specs=_x_spec(),
        out_shape=jax.ShapeDtypeStruct(x.shape, x.dtype),
        scratch_shapes=[
            pltpu.VMEM((N_SLABS, ROWS, LANES), jnp.float32),
            pltpu.VMEM((ROWS, D_MODEL), jnp.bfloat16),
            pltpu.VMEM(((FFN_CONV - 1) * BATCH, 2 * FFN_DFF), jnp.float32),
        ],
        compiler_params=_compiler_params(),
        name="channel_mixer",
    )(x, *params)


def _lru_gate_weights(w_a, w_x):
    def blockdiag(w):
        w = w.reshape(N_SLABS, 2, LRU_BW, LRU_BW)
        z = jnp.zeros((N_SLABS, LRU_BW, LRU_BW), w.dtype)
        top = jnp.concatenate([w[:, 0], z], axis=2)
        bot = jnp.concatenate([z, w[:, 1]], axis=2)
        return jnp.concatenate([top, bot], axis=1)
    return jnp.concatenate([blockdiag(w_a), blockdiag(w_x)], axis=2).astype(jnp.bfloat16)


def kernel(x, norm_mix, w_in, w_alpha, b_alpha, gla_norm, w_out_gla, lru_conv_w, lru_conv_b,
           lru_w_a, lru_b_a, lru_w_x, lru_b_x, lru_lambda, w_out_lru, w_o, norm_ffn, w_up,
           ffn_conv_w, ffn_conv_b, w_down, norm_final):
    depth = w_in.shape[0]
    bf = jnp.bfloat16
    row = lambda a: a.reshape(1, -1)
    points = [0]
    for width in IN_SPLITS:
        points.append(points[-1] + width)
    for l in range(depth):
        wq, wk, wv, wg, wac, wxr, wgr, wga, wgb = (
            w_in[l, :, points[i]:points[i + 1]].astype(bf) for i in range(len(IN_SPLITS)))
        wac = jnp.pad(wac, ((0, 0), (0, LANES - GLA_RANK)))
        walpha = jnp.pad(w_alpha[l].astype(bf), ((0, LANES - GLA_RANK), (0, 0)))
        mixer_params = (
            row(norm_mix[l]), wq, wk, wv, wg, wac, walpha, row(b_alpha[l]),
            row(jnp.tile(gla_norm[l], GLA_HEADS)), w_out_gla[l].astype(bf),
            wxr, wgr, lru_conv_w[l], row(lru_conv_b[l]),
            _lru_gate_weights(lru_w_a[l], lru_w_x[l]), row(lru_b_a[l]), row(lru_b_x[l]),
            row(lru_lambda[l]), w_out_lru[l].astype(bf), wga, wgb, w_o[l].astype(bf))
        x = _mixer_call(x, mixer_params)
        ffn_params = (row(norm_ffn[l]), w_up[l].astype(bf), ffn_conv_w[l], row(ffn_conv_b[l]),
                      w_down[l].astype(bf), row(norm_final))
        x = _ffn_call(x, ffn_params, final_norm=(l == depth - 1))
    return x
```

```python
import functools

import jax
import jax.numpy as jnp
from jax import lax
from jax.experimental import pallas as pl
from jax.experimental.pallas import tpu as pltpu

D_MODEL = 1024
BATCH = 8
CHUNK = 64
ROWS = BATCH * CHUNK
EPS = 1e-6
GLA_HEADS = 4
GLA_DK = 128
GLA_DV = 256
GLA_KW = GLA_HEADS * GLA_DK
GLA_VW = GLA_HEADS * GLA_DV
GLA_RANK = 16
GLA_TAU = 16.0
LRU_BLOCKS = 16
LRU_BW = 64
LRU_CONV = 4
LRU_C = 8.0
FFN_DFF = 3 * D_MODEL
FFN_CONV = 3
FFN_TILE = 512

LANES = 128
SUBLANES = 8
N_SLABS = D_MODEL // LANES
N_GROUPS = N_SLABS // 2
VMEM_LIMIT_BYTES = 58 * 1024 * 1024

IN_SPLITS = (GLA_KW, GLA_KW, GLA_VW, GLA_VW, GLA_RANK, D_MODEL, D_MODEL, D_MODEL, D_MODEL)
PACKED_ORDER = ("q", "k", "v", "g", "xr", "gr", "ga", "gb", "code")
PACKED_SOURCE = {"q": 0, "k": 1, "v": 2, "g": 3, "code": 4, "xr": 5, "gr": 6, "ga": 7, "gb": 8}
PACKED_WIDTH = {name: (LANES if name == "code" else IN_SPLITS[PACKED_SOURCE[name]])
                for name in PACKED_ORDER}
PACKED_START = {}
_col = 0
for _name in PACKED_ORDER:
    PACKED_START[_name] = _col
    _col += PACKED_WIDTH[_name]
PACKED_COLS = _col

assert BATCH == SUBLANES


def _bdot(a, b):
    return jnp.dot(a.astype(jnp.bfloat16), b.astype(jnp.bfloat16),
                   preferred_element_type=jnp.float32)


def _batch_rows(b):
    return pl.ds(b, CHUNK, stride=BATCH)


def _load_time_major(x_ref, slab_ref):
    for b in range(BATCH):
        xb = x_ref[b]
        for s in range(N_SLABS):
            slab_ref[s, _batch_rows(b), :] = xb[:, s * LANES:(s + 1) * LANES]
    return jnp.concatenate([slab_ref[s] for s in range(N_SLABS)], axis=1)


def _store_batch_major(val, slab_ref, out_ref):
    for s in range(N_SLABS):
        slab_ref[s] = val[:, s * LANES:(s + 1) * LANES]
    for b in range(BATCH):
        out_ref[b] = jnp.concatenate(
            [slab_ref[s, _batch_rows(b), :] for s in range(N_SLABS)], axis=1)


def _rmsnorm(x, g):
    return x * lax.rsqrt(jnp.mean(x * x, axis=-1, keepdims=True) + EPS) * g


def _log_sigmoid(z):
    return jnp.minimum(z, 0.0) - jnp.log1p(jnp.exp(-jnp.abs(z)))


def _causal_conv(xpad, w_ref, b_ref, width):
    y = b_ref[...]
    for j in range(width):
        y = y + xpad[j * BATCH:j * BATCH + ROWS] * w_ref[j:j + 1, :]
    return y


def _group(c):
    return slice(2 * c * LANES, (2 * c + 2) * LANES)


def _mixer_kernel(*refs, batch_major_in):
    refs = list(refs)
    x_ref = refs.pop(0)
    xn_ref = None if batch_major_in else refs.pop(0)
    (g_ref, win_ref, walpha_ref, balpha_ref, gnorm_ref, wouta_ref, cw_ref, cb_ref,
     wlru_ref, ba_ref, bx_ref, lam_ref, woutb_ref, wo_ref, out_ref,
     hn_ref, q_ref, kd_ref, etot_ref, v_ref, o_ref, st_ref, tail_ref, h_ref,
     *relayout_scratch) = refs
    step = pl.program_id(0)

    @pl.when(step == 0)
    def _():
        st_ref[...] = jnp.zeros_like(st_ref)
        tail_ref[...] = jnp.zeros_like(tail_ref)
        h_ref[...] = jnp.zeros_like(h_ref)
        if not batch_major_in:
            hn_ref[0] = _rmsnorm(x_ref[...], g_ref[...]).astype(jnp.bfloat16)

    if batch_major_in:
        (xs_ref,) = relayout_scratch
        hn_ref[0] = _rmsnorm(_load_time_major(x_ref, xs_ref), g_ref[...]).astype(jnp.bfloat16)
        hn_cur = hn_ref.at[0]
    else:
        slot = lax.rem(step, 2)
        hn_cur = hn_ref.at[slot]

    def proj(name, c=None):
        lo = PACKED_START[name]
        cols = slice(lo, lo + PACKED_WIDTH[name]) if c is None else slice(
            lo + 2 * c * LANES, lo + (2 * c + 2) * LANES)
        return _bdot(hn_cur[...], win_ref[:, cols])

    xr = proj("xr")
    code = proj("code")
    k = proj("k")
    la = _log_sigmoid(_bdot(code, walpha_ref[...]) + balpha_ref[...]) / GLA_TAU
    q = proj("q") * (GLA_DK ** -0.5)
    v = proj("v")
    for s in range(N_SLABS):
        v_ref[s] = v[:, s * LANES:(s + 1) * LANES]

    xpad = jnp.concatenate([tail_ref[...], xr], axis=0)
    tail_ref[...] = xr[ROWS - (LRU_CONV - 1) * BATCH:]
    xc = _causal_conv(xpad, cw_ref, cb_ref, LRU_CONV)
    gates = [_bdot(xc[:, s * LANES:(s + 1) * LANES], wlru_ref[s]) for s in range(N_SLABS)]

    run = jnp.zeros((BATCH, GLA_KW), jnp.float32)
    cums = []
    for t in range(CHUNK):
        run = run + la[t * BATCH:(t + 1) * BATCH]
        cums.append(run)
    etot_ref[...] = jnp.exp(run)
    kdec = k * jnp.exp(jnp.concatenate([run - c for c in cums], axis=0))
    for h in range(GLA_HEADS):
        q_ref[h] = q[:, h * GLA_DK:(h + 1) * GLA_DK]
        kd_ref[h] = kdec[:, h * GLA_DK:(h + 1) * GLA_DK]

    filler = [(name, c) for name in ("g", "gr", "ga", "gb") for c in range(N_GROUPS)]
    pieces = []
    for b in range(BATCH):
        rows = _batch_rows(b)
        for h in range(GLA_HEADS):
            kdb = kd_ref[h, rows, :]
            vb = jnp.concatenate([v_ref[2 * h, rows, :], v_ref[2 * h + 1, rows, :]], axis=1)
            decay = etot_ref[b:b + 1, h * GLA_DK:(h + 1) * GLA_DK]
            st_ref[b * GLA_HEADS + h] = st_ref[b * GLA_HEADS + h] * decay + lax.dot_general(
                vb.astype(jnp.bfloat16), kdb.astype(jnp.bfloat16),
                (((0,), (0,)), ((), ())), preferred_element_type=jnp.float32)
        pieces.append(proj(*filler[b]))
    for b in range(BATCH):
        rows = _batch_rows(b)
        for h in range(GLA_HEADS):
            o = lax.dot_general(q_ref[h, rows, :].astype(jnp.bfloat16),
                                st_ref[b * GLA_HEADS + h].astype(jnp.bfloat16),
                                (((1,), (1,)), ((), ())),
                                preferred_element_type=jnp.float32)
            o_ref[2 * h, rows, :] = o[:, :LANES]
            o_ref[2 * h + 1, rows, :] = o[:, LANES:]
        pieces.append(proj(*filler[BATCH + b]))
    g_out, gr, gate_a, gate_b = (
        jnp.concatenate(pieces[i * N_GROUPS:(i + 1) * N_GROUPS], axis=1) for i in range(4))

    if not batch_major_in:
        hn_ref[1 - slot] = _rmsnorm(xn_ref[...], g_ref[...]).astype(jnp.bfloat16)

    log_lam = LRU_C * _log_sigmoid(lam_ref[...])
    hr_tiles = []
    for s in range(N_SLABS):
        lanes = slice(s * LANES, (s + 1) * LANES)
        r = jax.nn.sigmoid(gates[s][:, :LANES] + ba_ref[:, lanes])
        i = jax.nn.sigmoid(gates[s][:, LANES:] + bx_ref[:, lanes])
        log_a = r * log_lam[:, lanes]
        a = jnp.exp(log_a)
        u = jnp.sqrt(-jnp.tanh(log_a) * (a * a + 1.0)) * (i * xc[:, lanes])
        hcur = h_ref[:, lanes]
        hs = []
        for t in range(CHUNK):
            rws = slice(t * BATCH, (t + 1) * BATCH)
            hcur = a[rws] * hcur + u[rws]
            hs.append(hcur)
        h_ref[:, lanes] = hcur
        hr_tiles.append(jnp.concatenate(hs, axis=0))
    hr = jnp.concatenate(hr_tiles, axis=1)

    o_heads = []
    for h in range(GLA_HEADS):
        oh = jnp.concatenate([o_ref[2 * h], o_ref[2 * h + 1]], axis=1)
        o_heads.append(oh * lax.rsqrt(jnp.mean(oh * oh, axis=-1, keepdims=True) + EPS))
    o = jnp.concatenate(o_heads, axis=1)
    y_a = _bdot(o * gnorm_ref[...] * jax.nn.silu(g_out), wouta_ref[...])
    y_b = _bdot(hr * jax.nn.gelu(gr), woutb_ref[...])

    merged = (jax.nn.sigmoid(gate_a) * y_a + jax.nn.sigmoid(gate_b) * y_b).astype(jnp.bfloat16)
    for c in range(N_GROUPS):
        cols = _group(c)
        if batch_major_in:
            res = jnp.concatenate([xs_ref[2 * c], xs_ref[2 * c + 1]], axis=1)
        else:
            res = x_ref[:, cols]
        out_ref[:, cols] = res + _bdot(merged, wo_ref[:, cols])


def _ffn_kernel(x_ref, xn_ref, g_ref, wup_ref, cw_ref, cb_ref, wdown_ref, gfin_ref, out_ref,
                hn_ref, tail_ref, *relayout_scratch, last):
    step = pl.program_id(0)
    slot = lax.rem(step, 2)

    @pl.when(step == 0)
    def _():
        tail_ref[...] = jnp.zeros_like(tail_ref)
        hn_ref[0] = _rmsnorm(x_ref[...], g_ref[...]).astype(jnp.bfloat16)

    hn_cur = hn_ref.at[slot]
    ctx = (FFN_CONV - 1) * BATCH
    n_tiles = FFN_DFF // FFN_TILE

    def up_tile(j):
        act = []
        for base in (j * FFN_TILE, FFN_DFF + j * FFN_TILE):
            cols = slice(base, base + FFN_TILE)
            up = _bdot(hn_cur[...], wup_ref[:, cols])
            upad = jnp.concatenate([tail_ref[:, cols], up], axis=0)
            tail_ref[:, cols] = up[ROWS - ctx:]
            act.append(_causal_conv(upad, cw_ref.at[:, cols], cb_ref.at[:, cols], FFN_CONV))
        val, gate = act
        return (jax.nn.gelu(gate) * val).astype(jnp.bfloat16)

    def down_rows(j):
        return slice(j * FFN_TILE, (j + 1) * FFN_TILE)

    acc = None
    act = up_tile(0)
    for j in range(1, n_tiles):
        act_next = up_tile(j)
        part = _bdot(act, wdown_ref[down_rows(j - 1), :])
        acc = part if acc is None else acc + part
        act = act_next
    hn_ref[1 - slot] = _rmsnorm(xn_ref[...], g_ref[...]).astype(jnp.bfloat16)
    tail_rows = down_rows(n_tiles - 1)
    if last:
        (xs_ref,) = relayout_scratch
        y = x_ref[...] + acc + _bdot(act, wdown_ref[tail_rows, :])
        _store_batch_major(_rmsnorm(y, gfin_ref[...]), xs_ref, out_ref)
    else:
        for c in range(N_GROUPS):
            cols = _group(c)
            out_ref[:, cols] = (x_ref[:, cols] + acc[:, cols]
                                + _bdot(act, wdown_ref[tail_rows, cols]))


def _layer_block(arr, layer):
    shape = arr.shape[1:]
    zeros = (0,) * len(shape)
    return pl.BlockSpec((None,) + shape, lambda i: (layer,) + zeros,
                        pipeline_mode=pl.Buffered(1))


def _resident(arr):
    zeros = (0,) * arr.ndim
    return pl.BlockSpec(arr.shape, lambda i: zeros, pipeline_mode=pl.Buffered(1))


def _x_spec(batch_major, n_steps=None, lookahead=0):
    if batch_major:
        return pl.BlockSpec((BATCH, CHUNK, D_MODEL), lambda i: (0, i, 0))
    if lookahead:
        return pl.BlockSpec((ROWS, D_MODEL),
                            lambda i: (jnp.minimum(i + lookahead, n_steps - 1), 0))
    return pl.BlockSpec((ROWS, D_MODEL), lambda i: (i, 0))


def _relayout_scratch(needed):
    return [pltpu.VMEM((N_SLABS, ROWS, LANES), jnp.float32)] if needed else []


def _compiler_params():
    return pltpu.CompilerParams(dimension_semantics=("arbitrary",),
                                vmem_limit_bytes=VMEM_LIMIT_BYTES)


def _mixer_call(x, params, layer, batch_major_in):
    n_rows = x.size // D_MODEL
    n_steps = n_rows // ROWS
    slab = pltpu.VMEM((N_SLABS, ROWS, LANES), jnp.float32)
    head_slab = pltpu.VMEM((GLA_HEADS, ROWS, LANES), jnp.float32)
    x_args, x_specs = [x], [_x_spec(batch_major_in)]
    if not batch_major_in:
        x_args.append(x)
        x_specs.append(_x_spec(False, n_steps, lookahead=1))
    return pl.pallas_call(
        functools.partial(_mixer_kernel, batch_major_in=batch_major_in),
        grid=(n_steps,),
        in_specs=x_specs + [_layer_block(p, layer) for p in params],
        out_specs=_x_spec(False),
        out_shape=jax.ShapeDtypeStruct((n_rows, D_MODEL), x.dtype),
        scratch_shapes=[
            pltpu.VMEM((1 if batch_major_in else 2, ROWS, D_MODEL), jnp.bfloat16),
            head_slab, head_slab,
            pltpu.VMEM((BATCH, GLA_KW), jnp.float32),
            slab, slab,
            pltpu.VMEM((BATCH * GLA_HEADS, GLA_DV, GLA_DK), jnp.float32),
            pltpu.VMEM(((LRU_CONV - 1) * BATCH, D_MODEL), jnp.float32),
            pltpu.VMEM((BATCH, D_MODEL), jnp.float32),
        ] + _relayout_scratch(batch_major_in),
        compiler_params=_compiler_params(),
        name="token_mixer",
    )(*x_args, *params)


def _ffn_call(x, params, gfin, layer, last):
    n_rows = x.shape[0]
    n_steps = n_rows // ROWS
    out_shape = (BATCH, n_rows // BATCH, D_MODEL) if last else (n_rows, D_MODEL)
    return pl.pallas_call(
        functools.partial(_ffn_kernel, last=last),
        grid=(n_steps,),
        in_specs=([_x_spec(False), _x_spec(False, n_steps, lookahead=1)]
                  + [_layer_block(p, layer) for p in params] + [_resident(gfin)]),
        out_specs=_x_spec(last),
        out_shape=jax.ShapeDtypeStruct(out_shape, x.dtype),
        scratch_shapes=[
            pltpu.VMEM((2, ROWS, D_MODEL), jnp.bfloat16),
            pltpu.VMEM(((FFN_CONV - 1) * BATCH, 2 * FFN_DFF), jnp.float32),
        ] + _relayout_scratch(last),
        compiler_params=_compiler_params(),
        name="channel_mixer",
    )(x, x, *params, gfin)


def _lru_gate_weights(w_a, w_x):
    def blockdiag(w):
        w = w.reshape(w.shape[0], N_SLABS, 2, LRU_BW, LRU_BW)
        z = jnp.zeros_like(w[:, :, 0])
        top = jnp.concatenate([w[:, :, 0], z], axis=3)
        bot = jnp.concatenate([z, w[:, :, 1]], axis=3)
        return jnp.concatenate([top, bot], axis=2)
    return jnp.concatenate([blockdiag(w_a), blockdiag(w_x)], axis=3).astype(jnp.bfloat16)


def _pack_w_in(w_in):
    points = [0]
    for width in IN_SPLITS:
        points.append(points[-1] + width)
    parts = []
    for name in PACKED_ORDER:
        i = PACKED_SOURCE[name]
        seg = w_in[:, :, points[i]:points[i + 1]]
        pad = PACKED_WIDTH[name] - seg.shape[2]
        parts.append(jnp.pad(seg, ((0, 0), (0, 0), (0, pad))) if pad else seg)
    return jnp.concatenate(parts, axis=2).astype(jnp.bfloat16)


def kernel(x, norm_mix, w_in, w_alpha, b_alpha, gla_norm, w_out_gla, lru_conv_w, lru_conv_b,
           lru_w_a, lru_b_a, lru_w_x, lru_b_x, lru_lambda, w_out_lru, w_o, norm_ffn, w_up,
           ffn_conv_w, ffn_conv_b, w_down, norm_final):
    depth = w_in.shape[0]
    bf = jnp.bfloat16
    rows = lambda a: a.reshape(a.shape[0], 1, -1)
    mixer_params = (
        rows(norm_mix), _pack_w_in(w_in),
        jnp.pad(w_alpha.astype(bf), ((0, 0), (0, LANES - GLA_RANK), (0, 0))), rows(b_alpha),
        rows(jnp.tile(gla_norm, (1, GLA_HEADS))), w_out_gla.astype(bf),
        lru_conv_w, rows(lru_conv_b), _lru_gate_weights(lru_w_a, lru_w_x),
        rows(lru_b_a), rows(lru_b_x), rows(lru_lambda), w_out_lru.astype(bf), w_o.astype(bf))
    ffn_params = (rows(norm_ffn), w_up.astype(bf), ffn_conv_w, rows(ffn_conv_b),
                  w_down.astype(bf))
    gfin = norm_final.reshape(1, -1)
    for l in range(depth):
        x = _mixer_call(x, mixer_params, l, batch_major_in=(l == 0))
        x = _ffn_call(x, ffn_params, gfin, l, last=(l == depth - 1))
    return x
```

```python
import functools

import jax
import jax.numpy as jnp
from jax import lax
from jax.experimental import pallas as pl
from jax.experimental.pallas import tpu as pltpu

D_MODEL = 1024
BATCH = 8
CHUNK = 64
ROWS = BATCH * CHUNK
EPS = 1e-6
GLA_HEADS = 4
GLA_DK = 128
GLA_DV = 256
GLA_KW = GLA_HEADS * GLA_DK
GLA_VW = GLA_HEADS * GLA_DV
GLA_RANK = 16
GLA_TAU = 16.0
LRU_BLOCKS = 16
LRU_BW = 64
LRU_CONV = 4
LRU_C = 8.0
FFN_DFF = 3 * D_MODEL
FFN_CONV = 3
FFN_TILE = 512

LANES = 128
SUBLANES = 8
N_SLABS = D_MODEL // LANES
N_GROUPS = N_SLABS // 2
VMEM_LIMIT_BYTES = 58 * 1024 * 1024

IN_SPLITS = (GLA_KW, GLA_KW, GLA_VW, GLA_VW, GLA_RANK, D_MODEL, D_MODEL, D_MODEL, D_MODEL)
IN_NAMES = ("q", "k", "v", "g", "code", "xr", "gr", "ga", "gb")
IN_START = {name: sum(IN_SPLITS[:i]) for i, name in enumerate(IN_NAMES)}
IN_WIDTH = dict(zip(IN_NAMES, IN_SPLITS))
IN_PARTS = (("q", "k", "v", "g"), ("xr", "gr", "ga", "gb"), ("code",))

assert BATCH == SUBLANES


def _bdot(a, b):
    return jnp.dot(a.astype(jnp.bfloat16), b.astype(jnp.bfloat16),
                   preferred_element_type=jnp.float32)


def _batch_rows(b):
    return pl.ds(b, CHUNK, stride=BATCH)


def _load_time_major(x_ref, slab_ref):
    for b in range(BATCH):
        xb = x_ref[b]
        for s in range(N_SLABS):
            slab_ref[s, _batch_rows(b), :] = xb[:, s * LANES:(s + 1) * LANES]
    return jnp.concatenate([slab_ref[s] for s in range(N_SLABS)], axis=1)


def _store_batch_major(val, slab_ref, out_ref):
    for s in range(N_SLABS):
        slab_ref[s] = val[:, s * LANES:(s + 1) * LANES]
    for b in range(BATCH):
        out_ref[b] = jnp.concatenate(
            [slab_ref[s, _batch_rows(b), :] for s in range(N_SLABS)], axis=1)


def _rmsnorm(x, g):
    return x * lax.rsqrt(jnp.mean(x * x, axis=-1, keepdims=True) + EPS) * g


def _log_sigmoid(z):
    return jnp.minimum(z, 0.0) - jnp.log1p(jnp.exp(-jnp.abs(z)))


def _causal_conv(xpad, w_ref, b_ref, width):
    y = b_ref[...]
    for j in range(width):
        y = y + xpad[j * BATCH:j * BATCH + ROWS] * w_ref[j:j + 1, :]
    return y


def _group(c):
    return slice(2 * c * LANES, (2 * c + 2) * LANES)


def _mixer_kernel(*refs, batch_major_in):
    refs = list(refs)
    x_ref = refs.pop(0)
    xn_ref = None if batch_major_in else refs.pop(0)
    (g_ref, wina_ref, winb_ref, winc_ref,
     walpha_ref, balpha_ref, gnorm_ref, wouta_ref, cw_ref, cb_ref,
     wlru_ref, ba_ref, bx_ref, lam_ref, woutb_ref, wo_ref, out_ref,
     hn_ref, q_ref, kd_ref, etot_ref, v_ref, o_ref, st_ref, tail_ref, h_ref,
     *relayout_scratch) = refs
    step = pl.program_id(0)

    @pl.when(step == 0)
    def _():
        st_ref[...] = jnp.zeros_like(st_ref)
        tail_ref[...] = jnp.zeros_like(tail_ref)
        h_ref[...] = jnp.zeros_like(h_ref)
        if not batch_major_in:
            hn_ref[0] = _rmsnorm(x_ref[...], g_ref[...]).astype(jnp.bfloat16)

    if batch_major_in:
        (xs_ref,) = relayout_scratch
        hn_ref[0] = _rmsnorm(_load_time_major(x_ref, xs_ref), g_ref[...]).astype(jnp.bfloat16)
        hn_cur = hn_ref.at[0]
    else:
        slot = lax.rem(step, 2)
        hn_cur = hn_ref.at[slot]

    def proj(name, c=None):
        part = next(i for i, names in enumerate(IN_PARTS) if name in names)
        w_ref = (wina_ref, winb_ref, winc_ref)[part]
        lo = IN_START[name] - IN_START[IN_PARTS[part][0]]
        width = w_ref.shape[1] if name == "code" else IN_WIDTH[name]
        cols = slice(lo, lo + width) if c is None else slice(
            lo + 2 * c * LANES, lo + (2 * c + 2) * LANES)
        return _bdot(hn_cur[...], w_ref[:, cols])

    xr = proj("xr")
    v = proj("v")
    for s in range(N_SLABS):
        v_ref[s] = v[:, s * LANES:(s + 1) * LANES]
    q = proj("q") * (GLA_DK ** -0.5)
    for h in range(GLA_HEADS):
        q_ref[h] = q[:, h * GLA_DK:(h + 1) * GLA_DK]

    xpad = jnp.concatenate([tail_ref[...], xr], axis=0)
    tail_ref[...] = xr[ROWS - (LRU_CONV - 1) * BATCH:]
    xc = _causal_conv(xpad, cw_ref, cb_ref, LRU_CONV)
    gates = [_bdot(xc[:, s * LANES:(s + 1) * LANES], wlru_ref[s]) for s in range(N_SLABS)]

    code = proj("code")
    k = proj("k")
    la = _log_sigmoid(_bdot(code, walpha_ref[...]) + balpha_ref[...]) / GLA_TAU
    g_out = jnp.concatenate([proj("g", c) for c in range(N_GROUPS)], axis=1)

    run = jnp.zeros((BATCH, GLA_KW), jnp.float32)
    cums = []
    for t in range(CHUNK):
        run = run + la[t * BATCH:(t + 1) * BATCH]
        cums.append(run)
    etot_ref[...] = jnp.exp(run)
    kdec = k * jnp.exp(jnp.concatenate([run - c for c in cums], axis=0))
    for h in range(GLA_HEADS):
        kd_ref[h] = kdec[:, h * GLA_DK:(h + 1) * GLA_DK]

    filler = [(name, c) for name in ("gr", "ga", "gb") for c in range(N_GROUPS)]
    pieces = []
    for b in range(BATCH):
        rows = _batch_rows(b)
        for h in range(GLA_HEADS):
            kdb = kd_ref[h, rows, :]
            vb = jnp.concatenate([v_ref[2 * h, rows, :], v_ref[2 * h + 1, rows, :]], axis=1)
            decay = etot_ref[b:b + 1, h * GLA_DK:(h + 1) * GLA_DK]
            st_ref[b * GLA_HEADS + h] = st_ref[b * GLA_HEADS + h] * decay + lax.dot_general(
                vb.astype(jnp.bfloat16), kdb.astype(jnp.bfloat16),
                (((0,), (0,)), ((), ())), preferred_element_type=jnp.float32)
        pieces.append(proj(*filler[b]))
    for b in range(BATCH):
        rows = _batch_rows(b)
        for h in range(GLA_HEADS):
            o = lax.dot_general(q_ref[h, rows, :].astype(jnp.bfloat16),
                                st_ref[b * GLA_HEADS + h].astype(jnp.bfloat16),
                                (((1,), (1,)), ((), ())),
                                preferred_element_type=jnp.float32)
            o_ref[2 * h, rows, :] = o[:, :LANES]
            o_ref[2 * h + 1, rows, :] = o[:, LANES:]
        if b % 2 == 1:
            pieces.append(proj(*filler[BATCH + b // 2]))
    gr, gate_a, gate_b = (
        jnp.concatenate(pieces[i * N_GROUPS:(i + 1) * N_GROUPS], axis=1) for i in range(3))

    if not batch_major_in:
        hn_ref[1 - slot] = _rmsnorm(xn_ref[...], g_ref[...]).astype(jnp.bfloat16)

    log_lam = LRU_C * _log_sigmoid(lam_ref[...])
    hr_tiles = []
    for s in range(N_SLABS):
        lanes = slice(s * LANES, (s + 1) * LANES)
        r = jax.nn.sigmoid(gates[s][:, :LANES] + ba_ref[:, lanes])
        i = jax.nn.sigmoid(gates[s][:, LANES:] + bx_ref[:, lanes])
        log_a = r * log_lam[:, lanes]
        a = jnp.exp(log_a)
        u = jnp.sqrt(-jnp.tanh(log_a) * (a * a + 1.0)) * (i * xc[:, lanes])
        hcur = h_ref[:, lanes]
        hs = []
        for t in range(CHUNK):
            rws = slice(t * BATCH, (t + 1) * BATCH)
            hcur = a[rws] * hcur + u[rws]
            hs.append(hcur)
        h_ref[:, lanes] = hcur
        hr_tiles.append(jnp.concatenate(hs, axis=0))
    hr = jnp.concatenate(hr_tiles, axis=1)

    o_heads = []
    for h in range(GLA_HEADS):
        oh = jnp.concatenate([o_ref[2 * h], o_ref[2 * h + 1]], axis=1)
        o_heads.append(oh * lax.rsqrt(jnp.mean(oh * oh, axis=-1, keepdims=True) + EPS))
    o = jnp.concatenate(o_heads, axis=1)
    y_a = _bdot(o * gnorm_ref[...] * jax.nn.silu(g_out), wouta_ref[...])
    y_b = _bdot(hr * jax.nn.gelu(gr), woutb_ref[...])

    merged = (jax.nn.sigmoid(gate_a) * y_a + jax.nn.sigmoid(gate_b) * y_b).astype(jnp.bfloat16)
    for c in range(N_GROUPS):
        cols = _group(c)
        if batch_major_in:
            res = jnp.concatenate([xs_ref[2 * c], xs_ref[2 * c + 1]], axis=1)
        else:
            res = x_ref[:, cols]
        out_ref[:, cols] = res + _bdot(merged, wo_ref[:, cols])


def _ffn_kernel(x_ref, xn_ref, g_ref, wup_ref, cw_ref, cb_ref, wdown_ref, gfin_ref, out_ref,
                hn_ref, tail_ref, act_ref, *relayout_scratch, last):
    step = pl.program_id(0)
    slot = lax.rem(step, 2)

    @pl.when(step == 0)
    def _():
        tail_ref[...] = jnp.zeros_like(tail_ref)
        hn_ref[0] = _rmsnorm(x_ref[...], g_ref[...]).astype(jnp.bfloat16)

    hn_cur = hn_ref.at[slot]
    ctx = (FFN_CONV - 1) * BATCH
    n_tiles = FFN_DFF // FFN_TILE

    def up_tile(j):
        act = []
        for base in (j * FFN_TILE, FFN_DFF + j * FFN_TILE):
            cols = slice(base, base + FFN_TILE)
            up = _bdot(hn_cur[...], wup_ref[:, cols])
            upad = jnp.concatenate([tail_ref[:, cols], up], axis=0)
            tail_ref[:, cols] = up[ROWS - ctx:]
            act.append(_causal_conv(upad, cw_ref.at[:, cols], cb_ref.at[:, cols], FFN_CONV))
        val, gate = act
        return (jax.nn.gelu(gate) * val).astype(jnp.bfloat16)

    for j in range(n_tiles):
        act_ref[:, j * FFN_TILE:(j + 1) * FFN_TILE] = up_tile(j)
    hn_ref[1 - slot] = _rmsnorm(xn_ref[...], g_ref[...]).astype(jnp.bfloat16)
    if last:
        (xs_ref,) = relayout_scratch
        y = x_ref[...] + _bdot(act_ref[...], wdown_ref[...])
        _store_batch_major(_rmsnorm(y, gfin_ref[...]), xs_ref, out_ref)
    else:
        for c in range(N_GROUPS):
            cols = _group(c)
            out_ref[:, cols] = x_ref[:, cols] + _bdot(act_ref[...], wdown_ref[:, cols])


def _layer_block(arr, layer):
    shape = arr.shape[1:]
    zeros = (0,) * len(shape)
    return pl.BlockSpec((None,) + shape, lambda i: (layer,) + zeros,
                        pipeline_mode=pl.Buffered(1))


def _resident(arr):
    zeros = (0,) * arr.ndim
    return pl.BlockSpec(arr.shape, lambda i: zeros, pipeline_mode=pl.Buffered(1))


def _x_spec(batch_major, n_steps=None, lookahead=0):
    if batch_major:
        return pl.BlockSpec((BATCH, CHUNK, D_MODEL), lambda i: (0, i, 0))
    if lookahead:
        return pl.BlockSpec((ROWS, D_MODEL),
                            lambda i: (jnp.minimum(i + lookahead, n_steps - 1), 0))
    return pl.BlockSpec((ROWS, D_MODEL), lambda i: (i, 0))


def _relayout_scratch(needed):
    return [pltpu.VMEM((N_SLABS, ROWS, LANES), jnp.float32)] if needed else []


def _compiler_params():
    return pltpu.CompilerParams(dimension_semantics=("arbitrary",),
                                vmem_limit_bytes=VMEM_LIMIT_BYTES)


def _mixer_call(x, params, layer, batch_major_in):
    n_rows = x.size // D_MODEL
    n_steps = n_rows // ROWS
    slab = pltpu.VMEM((N_SLABS, ROWS, LANES), jnp.float32)
    head_slab = pltpu.VMEM((GLA_HEADS, ROWS, LANES), jnp.float32)
    x_args, x_specs = [x], [_x_spec(batch_major_in)]
    if not batch_major_in:
        x_args.append(x)
        x_specs.append(_x_spec(False, n_steps, lookahead=1))
    return pl.pallas_call(
        functools.partial(_mixer_kernel, batch_major_in=batch_major_in),
        grid=(n_steps,),
        in_specs=x_specs + [_layer_block(p, layer) for p in params],
        out_specs=_x_spec(False),
        out_shape=jax.ShapeDtypeStruct((n_rows, D_MODEL), x.dtype),
        scratch_shapes=[
            pltpu.VMEM((1 if batch_major_in else 2, ROWS, D_MODEL), jnp.bfloat16),
            head_slab, head_slab,
            pltpu.VMEM((BATCH, GLA_KW), jnp.float32),
            slab, slab,
            pltpu.VMEM((BATCH * GLA_HEADS, GLA_DV, GLA_DK), jnp.float32),
            pltpu.VMEM(((LRU_CONV - 1) * BATCH, D_MODEL), jnp.float32),
            pltpu.VMEM((BATCH, D_MODEL), jnp.float32),
        ] + _relayout_scratch(batch_major_in),
        compiler_params=_compiler_params(),
        name="token_mixer",
    )(*x_args, *params)


def _ffn_call(x, params, gfin, layer, last):
    n_rows = x.shape[0]
    n_steps = n_rows // ROWS
    out_shape = (BATCH, n_rows // BATCH, D_MODEL) if last else (n_rows, D_MODEL)
    return pl.pallas_call(
        functools.partial(_ffn_kernel, last=last),
        grid=(n_steps,),
        in_specs=([_x_spec(False), _x_spec(False, n_steps, lookahead=1)]
                  + [_layer_block(p, layer) for p in params] + [_resident(gfin)]),
        out_specs=_x_spec(last),
        out_shape=jax.ShapeDtypeStruct(out_shape, x.dtype),
        scratch_shapes=[
            pltpu.VMEM((2, ROWS, D_MODEL), jnp.bfloat16),
            pltpu.VMEM(((FFN_CONV - 1) * BATCH, 2 * FFN_DFF), jnp.float32),
            pltpu.VMEM((ROWS, FFN_DFF), jnp.bfloat16),
        ] + _relayout_scratch(last),
        compiler_params=_compiler_params(),
        name="channel_mixer",
    )(x, x, *params, gfin)


def _lru_gate_weights(w_a, w_x):
    def blockdiag(w):
        w = w.reshape(w.shape[0], N_SLABS, 2, LRU_BW, LRU_BW)
        z = jnp.zeros_like(w[:, :, 0])
        top = jnp.concatenate([w[:, :, 0], z], axis=3)
        bot = jnp.concatenate([z, w[:, :, 1]], axis=3)
        return jnp.concatenate([top, bot], axis=2)
    return jnp.concatenate([blockdiag(w_a), blockdiag(w_x)], axis=3).astype(jnp.bfloat16)


def _split_w_in(w_in):
    parts = []
    for names in IN_PARTS:
        lo = IN_START[names[0]]
        hi = IN_START[names[-1]] + IN_WIDTH[names[-1]]
        parts.append(w_in[:, :, lo:hi].astype(jnp.bfloat16))
    parts[2] = jnp.pad(parts[2], ((0, 0), (0, 0), (0, LANES - GLA_RANK)))
    return parts


def kernel(x, norm_mix, w_in, w_alpha, b_alpha, gla_norm, w_out_gla, lru_conv_w, lru_conv_b,
           lru_w_a, lru_b_a, lru_w_x, lru_b_x, lru_lambda, w_out_lru, w_o, norm_ffn, w_up,
           ffn_conv_w, ffn_conv_b, w_down, norm_final):
    depth = w_in.shape[0]
    bf = jnp.bfloat16
    rows = lambda a: a.reshape(a.shape[0], 1, -1)
    mixer_params = (
        rows(norm_mix), *_split_w_in(w_in),
        jnp.pad(w_alpha.astype(bf), ((0, 0), (0, LANES - GLA_RANK), (0, 0))), rows(b_alpha),
        rows(jnp.tile(gla_norm, (1, GLA_HEADS))), w_out_gla.astype(bf),
        lru_conv_w, rows(lru_conv_b), _lru_gate_weights(lru_w_a, lru_w_x),
        rows(lru_b_a), rows(lru_b_x), rows(lru_lambda), w_out_lru.astype(bf), w_o.astype(bf))
    ffn_params = (rows(norm_ffn), w_up.astype(bf), ffn_conv_w, rows(ffn_conv_b),
                  w_down.astype(bf))
    gfin = norm_final.reshape(1, -1)
    for l in range(depth):
        x = _mixer_call(x, mixer_params, l, batch_major_in=(l == 0))
        x = _ffn_call(x, ffn_params, gfin, l, last=(l == depth - 1))
    return x
```

```python
import functools

import jax
import jax.numpy as jnp
from jax import lax
from jax.experimental import pallas as pl
from jax.experimental.pallas import tpu as pltpu

D_MODEL = 1024
BATCH = 8
CHUNK = 64
ROWS = BATCH * CHUNK
EPS = 1e-6
GLA_HEADS = 4
GLA_DK = 128
GLA_DV = 256
GLA_KW = GLA_HEADS * GLA_DK
GLA_VW = GLA_HEADS * GLA_DV
GLA_RANK = 16
GLA_TAU = 16.0
LRU_BLOCKS = 16
LRU_BW = 64
LRU_CONV = 4
LRU_C = 8.0
FFN_DFF = 3 * D_MODEL
FFN_CONV = 3
FFN_TILE = 512

LANES = 128
SUBLANES = 8
N_SLABS = D_MODEL // LANES
N_GROUPS = N_SLABS // 2
VMEM_LIMIT_BYTES = 58 * 1024 * 1024

IN_SPLITS = (GLA_KW, GLA_KW, GLA_VW, GLA_VW, GLA_RANK, D_MODEL, D_MODEL, D_MODEL, D_MODEL)
IN_NAMES = ("q", "k", "v", "g", "code", "xr", "gr", "ga", "gb")
IN_START = {name: sum(IN_SPLITS[:i]) for i, name in enumerate(IN_NAMES)}
IN_WIDTH = dict(zip(IN_NAMES, IN_SPLITS))
WSLAB = 512
SLABBED = ("q", "k", "v", "g", "xr", "gr", "ga", "gb")
SLAB_BASE = {}
_n = 0
for _name in SLABBED:
    SLAB_BASE[_name] = _n
    _n += IN_WIDTH[_name] // WSLAB
GROUP = 2 * LANES

assert BATCH == SUBLANES


def _bdot(a, b):
    return jnp.dot(a.astype(jnp.bfloat16), b.astype(jnp.bfloat16),
                   preferred_element_type=jnp.float32)


def _batch_rows(b):
    return pl.ds(b, CHUNK, stride=BATCH)


def _load_time_major(x_ref, slab_ref):
    for b in range(BATCH):
        xb = x_ref[b]
        for s in range(N_SLABS):
            slab_ref[s, _batch_rows(b), :] = xb[:, s * LANES:(s + 1) * LANES]
    return jnp.concatenate([slab_ref[s] for s in range(N_SLABS)], axis=1)


def _store_batch_major(val, slab_ref, out_ref):
    for s in range(N_SLABS):
        slab_ref[s] = val[:, s * LANES:(s + 1) * LANES]
    for b in range(BATCH):
        out_ref[b] = jnp.concatenate(
            [slab_ref[s, _batch_rows(b), :] for s in range(N_SLABS)], axis=1)


def _rmsnorm(x, g):
    return x * lax.rsqrt(jnp.mean(x * x, axis=-1, keepdims=True) + EPS) * g


def _log_sigmoid(z):
    return jnp.minimum(z, 0.0) - jnp.log(1.0 + jnp.exp(-jnp.abs(z)))


def _sqrt_nonneg(y):
    return jnp.exp(0.5 * jnp.log(y))


def _causal_conv(xpad, w_ref, b_ref, width):
    y = b_ref[...]
    for j in range(width):
        y = y + xpad[j * BATCH:j * BATCH + ROWS] * w_ref[j:j + 1, :]
    return y


def _group(c):
    return slice(c * GROUP, (c + 1) * GROUP)


def _group_cols(w_ref, base, c):
    per_slab = WSLAB // GROUP
    lo = (c % per_slab) * GROUP
    return w_ref[base + c // per_slab, :, lo:lo + GROUP]


def _slab_dot(a, w_ref):
    return jnp.concatenate([_bdot(a, w_ref[i]) for i in range(w_ref.shape[0])], axis=1)


def _mixer_kernel(*refs, batch_major_in):
    refs = list(refs)
    x_ref = refs.pop(0)
    xn_ref = None if batch_major_in else refs.pop(0)
    (g_ref, win_ref, wcode_ref,
     walpha_ref, balpha_ref, gnorm_ref, wouta_ref, cw_ref, cb_ref,
     wlru_ref, ba_ref, bx_ref, lam_ref, woutb_ref, wo_ref, out_ref,
     hn_ref, q_ref, kd_ref, etot_ref, v_ref, o_ref, st_ref, tail_ref, h_ref,
     *relayout_scratch) = refs
    step = pl.program_id(0)

    @pl.when(step == 0)
    def _():
        st_ref[...] = jnp.zeros_like(st_ref)
        tail_ref[...] = jnp.zeros_like(tail_ref)
        h_ref[...] = jnp.zeros_like(h_ref)
        if not batch_major_in:
            hn_ref[0] = _rmsnorm(x_ref[...], g_ref[...]).astype(jnp.bfloat16)

    if batch_major_in:
        (xs_ref,) = relayout_scratch
        hn_ref[0] = _rmsnorm(_load_time_major(x_ref, xs_ref), g_ref[...]).astype(jnp.bfloat16)
        hn_cur = hn_ref.at[0]
    else:
        slot = lax.rem(step, 2)
        hn_cur = hn_ref.at[slot]

    def proj(name, c=None):
        if name == "code":
            return _bdot(hn_cur[...], wcode_ref[...])
        if c is not None:
            return _bdot(hn_cur[...], _group_cols(win_ref, SLAB_BASE[name], c))
        return jnp.concatenate(
            [_bdot(hn_cur[...], win_ref[SLAB_BASE[name] + i])
             for i in range(IN_WIDTH[name] // WSLAB)], axis=1)

    xr = proj("xr")
    v = proj("v")
    for s in range(N_SLABS):
        v_ref[s] = v[:, s * LANES:(s + 1) * LANES]
    q = proj("q") * (GLA_DK ** -0.5)
    for h in range(GLA_HEADS):
        q_ref[h] = q[:, h * GLA_DK:(h + 1) * GLA_DK]

    xpad = jnp.concatenate([tail_ref[...], xr], axis=0)
    tail_ref[...] = xr[ROWS - (LRU_CONV - 1) * BATCH:]
    xc = _causal_conv(xpad, cw_ref, cb_ref, LRU_CONV)
    gates = [_bdot(xc[:, s * LANES:(s + 1) * LANES], wlru_ref[s]) for s in range(N_SLABS)]

    code = proj("code")
    k = proj("k")
    la = _log_sigmoid(_bdot(code, walpha_ref[...]) + balpha_ref[...]) / GLA_TAU
    g_out = jnp.concatenate([proj("g", c) for c in range(N_GROUPS)], axis=1)

    run = jnp.zeros((BATCH, GLA_KW), jnp.float32)
    cums = []
    for t in range(CHUNK):
        run = run + la[t * BATCH:(t + 1) * BATCH]
        cums.append(run)
    etot_ref[...] = jnp.exp(run)
    kdec = k * jnp.exp(jnp.concatenate([run - c for c in cums], axis=0))
    for h in range(GLA_HEADS):
        kd_ref[h] = kdec[:, h * GLA_DK:(h + 1) * GLA_DK]

    filler = [(name, c) for name in ("gr", "ga", "gb") for c in range(N_GROUPS)]
    pieces = []
    for b in range(BATCH):
        rows = _batch_rows(b)
        for h in range(GLA_HEADS):
            kdb = kd_ref[h, rows, :]
            vb = jnp.concatenate([v_ref[2 * h, rows, :], v_ref[2 * h + 1, rows, :]], axis=1)
            decay = etot_ref[b:b + 1, h * GLA_DK:(h + 1) * GLA_DK]
            st_ref[b * GLA_HEADS + h] = st_ref[b * GLA_HEADS + h] * decay + lax.dot_general(
                vb.astype(jnp.bfloat16), kdb.astype(jnp.bfloat16),
                (((0,), (0,)), ((), ())), preferred_element_type=jnp.float32)
        pieces.append(proj(*filler[b]))
    for b in range(BATCH):
        rows = _batch_rows(b)
        for h in range(GLA_HEADS):
            o = lax.dot_general(q_ref[h, rows, :].astype(jnp.bfloat16),
                                st_ref[b * GLA_HEADS + h].astype(jnp.bfloat16),
                                (((1,), (1,)), ((), ())),
                                preferred_element_type=jnp.float32)
            o_ref[2 * h, rows, :] = o[:, :LANES]
            o_ref[2 * h + 1, rows, :] = o[:, LANES:]
        if b % 2 == 1:
            pieces.append(proj(*filler[BATCH + b // 2]))
    gr, gate_a, gate_b = (
        jnp.concatenate(pieces[i * N_GROUPS:(i + 1) * N_GROUPS], axis=1) for i in range(3))

    if not batch_major_in:
        hn_ref[1 - slot] = _rmsnorm(xn_ref[...], g_ref[...]).astype(jnp.bfloat16)

    log_lam = LRU_C * _log_sigmoid(lam_ref[...])
    hr_tiles = []
    for s in range(N_SLABS):
        lanes = slice(s * LANES, (s + 1) * LANES)
        r = jax.nn.sigmoid(gates[s][:, :LANES] + ba_ref[:, lanes])
        i = jax.nn.sigmoid(gates[s][:, LANES:] + bx_ref[:, lanes])
        log_a = r * log_lam[:, lanes]
        a = jnp.exp(log_a)
        u = _sqrt_nonneg(-jnp.tanh(log_a) * (a * a + 1.0)) * (i * xc[:, lanes])
        hcur = h_ref[:, lanes]
        hs = []
        for t in range(CHUNK):
            rws = slice(t * BATCH, (t + 1) * BATCH)
            hcur = a[rws] * hcur + u[rws]
            hs.append(hcur)
        h_ref[:, lanes] = hcur
        hr_tiles.append(jnp.concatenate(hs, axis=0))
    hr = jnp.concatenate(hr_tiles, axis=1)

    o_heads = []
    for h in range(GLA_HEADS):
        oh = jnp.concatenate([o_ref[2 * h], o_ref[2 * h + 1]], axis=1)
        o_heads.append(oh * lax.rsqrt(jnp.mean(oh * oh, axis=-1, keepdims=True) + EPS))
    o = jnp.concatenate(o_heads, axis=1)
    y_a = _slab_dot(o * gnorm_ref[...] * jax.nn.silu(g_out), wouta_ref)
    y_b = _slab_dot(hr * jax.nn.gelu(gr), woutb_ref)

    merged = (jax.nn.sigmoid(gate_a) * y_a + jax.nn.sigmoid(gate_b) * y_b).astype(jnp.bfloat16)
    for c in range(N_GROUPS):
        cols = _group(c)
        if batch_major_in:
            res = jnp.concatenate([xs_ref[2 * c], xs_ref[2 * c + 1]], axis=1)
        else:
            res = x_ref[:, cols]
        out_ref[:, cols] = res + _bdot(merged, _group_cols(wo_ref, 0, c))


def _ffn_kernel(x_ref, xn_ref, g_ref, wup_ref, cw_ref, cb_ref, wdown_ref, gfin_ref, out_ref,
                hn_ref, tail_ref, act_ref, *relayout_scratch, last):
    step = pl.program_id(0)
    slot = lax.rem(step, 2)

    @pl.when(step == 0)
    def _():
        tail_ref[...] = jnp.zeros_like(tail_ref)
        hn_ref[0] = _rmsnorm(x_ref[...], g_ref[...]).astype(jnp.bfloat16)

    hn_cur = hn_ref.at[slot]
    ctx = (FFN_CONV - 1) * BATCH
    n_tiles = FFN_DFF // FFN_TILE

    def up_tile(j):
        act = []
        for base in (j * FFN_TILE, FFN_DFF + j * FFN_TILE):
            cols = slice(base, base + FFN_TILE)
            up = _bdot(hn_cur[...], wup_ref[base // WSLAB])
            upad = jnp.concatenate([tail_ref[:, cols], up], axis=0)
            tail_ref[:, cols] = up[ROWS - ctx:]
            act.append(_causal_conv(upad, cw_ref.at[:, cols], cb_ref.at[:, cols], FFN_CONV))
        val, gate = act
        return (jax.nn.gelu(gate) * val).astype(jnp.bfloat16)

    for j in range(n_tiles):
        act_ref[:, j * FFN_TILE:(j + 1) * FFN_TILE] = up_tile(j)
    hn_ref[1 - slot] = _rmsnorm(xn_ref[...], g_ref[...]).astype(jnp.bfloat16)
    if last:
        (xs_ref,) = relayout_scratch
        y = x_ref[...] + _slab_dot(act_ref[...], wdown_ref)
        _store_batch_major(_rmsnorm(y, gfin_ref[...]), xs_ref, out_ref)
    else:
        for c in range(N_GROUPS):
            cols = _group(c)
            out_ref[:, cols] = x_ref[:, cols] + _bdot(act_ref[...], _group_cols(wdown_ref, 0, c))


def _layer_block(arr, layer):
    shape = arr.shape[1:]
    zeros = (0,) * len(shape)
    return pl.BlockSpec((None,) + shape, lambda i: (layer,) + zeros,
                        pipeline_mode=pl.Buffered(1))


def _resident(arr):
    zeros = (0,) * arr.ndim
    return pl.BlockSpec(arr.shape, lambda i: zeros, pipeline_mode=pl.Buffered(1))


def _x_spec(batch_major, n_steps=None, lookahead=0):
    if batch_major:
        return pl.BlockSpec((BATCH, CHUNK, D_MODEL), lambda i: (0, i, 0))
    if lookahead:
        return pl.BlockSpec((ROWS, D_MODEL),
                            lambda i: (jnp.minimum(i + lookahead, n_steps - 1), 0))
    return pl.BlockSpec((ROWS, D_MODEL), lambda i: (i, 0))


def _relayout_scratch(needed):
    return [pltpu.VMEM((N_SLABS, ROWS, LANES), jnp.float32)] if needed else []


def _compiler_params():
    return pltpu.CompilerParams(dimension_semantics=("arbitrary",),
                                vmem_limit_bytes=VMEM_LIMIT_BYTES)


def _mixer_call(x, params, layer, batch_major_in):
    n_rows = x.size // D_MODEL
    n_steps = n_rows // ROWS
    slab = pltpu.VMEM((N_SLABS, ROWS, LANES), jnp.float32)
    head_slab = pltpu.VMEM((GLA_HEADS, ROWS, LANES), jnp.float32)
    x_args, x_specs = [x], [_x_spec(batch_major_in)]
    if not batch_major_in:
        x_args.append(x)
        x_specs.append(_x_spec(False, n_steps, lookahead=1))
    return pl.pallas_call(
        functools.partial(_mixer_kernel, batch_major_in=batch_major_in),
        grid=(n_steps,),
        in_specs=x_specs + [_layer_block(p, layer) for p in params],
        out_specs=_x_spec(False),
        out_shape=jax.ShapeDtypeStruct((n_rows, D_MODEL), x.dtype),
        scratch_shapes=[
            pltpu.VMEM((1 if batch_major_in else 2, ROWS, D_MODEL), jnp.bfloat16),
            head_slab, head_slab,
            pltpu.VMEM((BATCH, GLA_KW), jnp.float32),
            slab, slab,
            pltpu.VMEM((BATCH * GLA_HEADS, GLA_DV, GLA_DK), jnp.float32),
            pltpu.VMEM(((LRU_CONV - 1) * BATCH, D_MODEL), jnp.float32),
            pltpu.VMEM((BATCH, D_MODEL), jnp.float32),
        ] + _relayout_scratch(batch_major_in),
        compiler_params=_compiler_params(),
        name="token_mixer",
    )(*x_args, *params)


def _ffn_call(x, params, gfin, layer, last):
    n_rows = x.shape[0]
    n_steps = n_rows // ROWS
    out_shape = (BATCH, n_rows // BATCH, D_MODEL) if last else (n_rows, D_MODEL)
    return pl.pallas_call(
        functools.partial(_ffn_kernel, last=last),
        grid=(n_steps,),
        in_specs=([_x_spec(False), _x_spec(False, n_steps, lookahead=1)]
                  + [_layer_block(p, layer) for p in params] + [_resident(gfin)]),
        out_specs=_x_spec(last),
        out_shape=jax.ShapeDtypeStruct(out_shape, x.dtype),
        scratch_shapes=[
            pltpu.VMEM((2, ROWS, D_MODEL), jnp.bfloat16),
            pltpu.VMEM(((FFN_CONV - 1) * BATCH, 2 * FFN_DFF), jnp.float32),
            pltpu.VMEM((ROWS, FFN_DFF), jnp.bfloat16),
        ] + _relayout_scratch(last),
        compiler_params=_compiler_params(),
        name="channel_mixer",
    )(x, x, *params, gfin)


def _lru_gate_weights(w_a, w_x):
    def blockdiag(w):
        w = w.reshape(w.shape[0], N_SLABS, 2, LRU_BW, LRU_BW)
        z = jnp.zeros_like(w[:, :, 0])
        top = jnp.concatenate([w[:, :, 0], z], axis=3)
        bot = jnp.concatenate([z, w[:, :, 1]], axis=3)
        return jnp.concatenate([top, bot], axis=2)
    return jnp.concatenate([blockdiag(w_a), blockdiag(w_x)], axis=3).astype(jnp.bfloat16)


def _slabs(w):
    n_layers, depth, width = w.shape
    w = w.reshape(n_layers, depth, width // WSLAB, WSLAB)
    return jnp.transpose(w, (0, 2, 1, 3)).astype(jnp.bfloat16)


def _split_w_in(w_in):
    code_lo = IN_START["code"]
    code_hi = code_lo + GLA_RANK
    main = jnp.concatenate([w_in[:, :, :code_lo], w_in[:, :, code_hi:]], axis=2)
    code = jnp.pad(w_in[:, :, code_lo:code_hi].astype(jnp.bfloat16),
                   ((0, 0), (0, 0), (0, LANES - GLA_RANK)))
    return _slabs(main), code


def kernel(x, norm_mix, w_in, w_alpha, b_alpha, gla_norm, w_out_gla, lru_conv_w, lru_conv_b,
           lru_w_a, lru_b_a, lru_w_x, lru_b_x, lru_lambda, w_out_lru, w_o, norm_ffn, w_up,
           ffn_conv_w, ffn_conv_b, w_down, norm_final):
    depth = w_in.shape[0]
    rows = lambda a: a.reshape(a.shape[0], 1, -1)
    walpha = jnp.pad(w_alpha.astype(jnp.bfloat16), ((0, 0), (0, LANES - GLA_RANK), (0, 0)))
    mixer_params = (
        rows(norm_mix), *_split_w_in(w_in), walpha, rows(b_alpha),
        rows(jnp.tile(gla_norm, (1, GLA_HEADS))), _slabs(w_out_gla),
        lru_conv_w, rows(lru_conv_b), _lru_gate_weights(lru_w_a, lru_w_x),
        rows(lru_b_a), rows(lru_b_x), rows(lru_lambda), _slabs(w_out_lru), _slabs(w_o))
    ffn_params = (rows(norm_ffn), _slabs(w_up), ffn_conv_w, rows(ffn_conv_b), _slabs(w_down))
    gfin = norm_final.reshape(1, -1)
    for l in range(depth):
        x = _mixer_call(x, mixer_params, l, batch_major_in=(l == 0))
        x = _ffn_call(x, ffn_params, gfin, l, last=(l == depth - 1))
    return x
```

```python
import functools

import jax
import jax.numpy as jnp
from jax import lax
from jax.experimental import pallas as pl
from jax.experimental.pallas import tpu as pltpu

D_MODEL = 1024
BATCH = 8
CHUNK = 64
ROWS = BATCH * CHUNK
EPS = 1e-6
GLA_HEADS = 4
GLA_DK = 128
GLA_DV = 256
GLA_KW = GLA_HEADS * GLA_DK
GLA_VW = GLA_HEADS * GLA_DV
GLA_RANK = 16
GLA_TAU = 16.0
LRU_BLOCKS = 16
LRU_BW = 64
LRU_CONV = 4
LRU_C = 8.0
FFN_DFF = 3 * D_MODEL
FFN_CONV = 3
FFN_TILE = 512

LANES = 128
SUBLANES = 8
N_SLABS = D_MODEL // LANES
N_GROUPS = N_SLABS // 2
VMEM_LIMIT_BYTES = 58 * 1024 * 1024

IN_SPLITS = (GLA_KW, GLA_KW, GLA_VW, GLA_VW, GLA_RANK, D_MODEL, D_MODEL, D_MODEL, D_MODEL)
PACKED_ORDER = ("q", "k", "v", "g", "xr", "gr", "ga", "gb", "code")
PACKED_SOURCE = {"q": 0, "k": 1, "v": 2, "g": 3, "code": 4, "xr": 5, "gr": 6, "ga": 7, "gb": 8}
PACKED_WIDTH = {name: (LANES if name == "code" else IN_SPLITS[PACKED_SOURCE[name]])
                for name in PACKED_ORDER}
PACKED_START = {}
_col = 0
for _name in PACKED_ORDER:
    PACKED_START[_name] = _col
    _col += PACKED_WIDTH[_name]
PACKED_COLS = _col

assert BATCH == SUBLANES


def _bdot(a, b):
    return jnp.dot(a.astype(jnp.bfloat16), b.astype(jnp.bfloat16),
                   preferred_element_type=jnp.float32)


def _batch_rows(b):
    return pl.ds(b, CHUNK, stride=BATCH)


def _load_time_major(x_ref, slab_ref):
    for b in range(BATCH):
        xb = x_ref[b]
        for s in range(N_SLABS):
            slab_ref[s, _batch_rows(b), :] = xb[:, s * LANES:(s + 1) * LANES]
    return jnp.concatenate([slab_ref[s] for s in range(N_SLABS)], axis=1)


def _store_batch_major(val, slab_ref, out_ref):
    for s in range(N_SLABS):
        slab_ref[s] = val[:, s * LANES:(s + 1) * LANES]
    for b in range(BATCH):
        out_ref[b] = jnp.concatenate(
            [slab_ref[s, _batch_rows(b), :] for s in range(N_SLABS)], axis=1)


def _rmsnorm(x, g):
    return x * lax.rsqrt(jnp.mean(x * x, axis=-1, keepdims=True) + EPS) * g


def _log_sigmoid(z):
    return jnp.minimum(z, 0.0) - jnp.log(1.0 + jnp.exp(-jnp.abs(z)))


def _sqrt_nonneg(y):
    return jnp.exp(0.5 * jnp.log(y))


def _causal_conv(xpad, w_ref, b_ref, width):
    y = b_ref[...]
    for j in range(width):
        y = y + xpad[j * BATCH:j * BATCH + ROWS] * w_ref[j:j + 1, :]
    return y


def _group(c):
    return slice(2 * c * LANES, (2 * c + 2) * LANES)


def _mixer_kernel(*refs, batch_major_in):
    refs = list(refs)
    x_ref = refs.pop(0)
    xn_ref = None if batch_major_in else refs.pop(0)
    (g_ref, win_ref, walpha_ref, balpha_ref, gnorm_ref, wouta_ref, cw_ref, cb_ref,
     wlru_ref, ba_ref, bx_ref, lam_ref, woutb_ref, wo_ref, out_ref,
     hn_ref, q_ref, kd_ref, etot_ref, v_ref, o_ref, st_ref, tail_ref, h_ref,
     *relayout_scratch) = refs
    step = pl.program_id(0)

    @pl.when(step == 0)
    def _():
        st_ref[...] = jnp.zeros_like(st_ref)
        tail_ref[...] = jnp.zeros_like(tail_ref)
        h_ref[...] = jnp.zeros_like(h_ref)
        if not batch_major_in:
            hn_ref[0] = _rmsnorm(x_ref[...], g_ref[...]).astype(jnp.bfloat16)

    if batch_major_in:
        (xs_ref,) = relayout_scratch
        hn_ref[0] = _rmsnorm(_load_time_major(x_ref, xs_ref), g_ref[...]).astype(jnp.bfloat16)
        hn_cur = hn_ref.at[0]
    else:
        slot = lax.rem(step, 2)
        hn_cur = hn_ref.at[slot]

    def proj(name, c=None):
        lo = PACKED_START[name]
        cols = slice(lo, lo + PACKED_WIDTH[name]) if c is None else slice(
            lo + 2 * c * LANES, lo + (2 * c + 2) * LANES)
        return _bdot(hn_cur[...], win_ref[:, cols])

    xr = proj("xr")
    v = proj("v")
    for s in range(N_SLABS):
        v_ref[s] = v[:, s * LANES:(s + 1) * LANES]
    q = proj("q") * (GLA_DK ** -0.5)
    for h in range(GLA_HEADS):
        q_ref[h] = q[:, h * GLA_DK:(h + 1) * GLA_DK]

    xpad = jnp.concatenate([tail_ref[...], xr], axis=0)
    tail_ref[...] = xr[ROWS - (LRU_CONV - 1) * BATCH:]
    xc = _causal_conv(xpad, cw_ref, cb_ref, LRU_CONV)
    gates = [_bdot(xc[:, s * LANES:(s + 1) * LANES], wlru_ref[s]) for s in range(N_SLABS)]

    code = proj("code")
    k = proj("k")
    la = _log_sigmoid(_bdot(code, walpha_ref[...]) + balpha_ref[...]) / GLA_TAU
    g_out = jnp.concatenate([proj("g", c) for c in range(N_GROUPS)], axis=1)

    run = jnp.zeros((BATCH, GLA_KW), jnp.float32)
    cums = []
    for t in range(CHUNK):
        run = run + la[t * BATCH:(t + 1) * BATCH]
        cums.append(run)
    etot_ref[...] = jnp.exp(run)
    kdec = k * jnp.exp(jnp.concatenate([run - c for c in cums], axis=0))
    for h in range(GLA_HEADS):
        kd_ref[h] = kdec[:, h * GLA_DK:(h + 1) * GLA_DK]

    filler = [(name, c) for name in ("gr", "ga", "gb") for c in range(N_GROUPS)]
    pieces = []
    for b in range(BATCH):
        rows = _batch_rows(b)
        for h in range(GLA_HEADS):
            kdb = kd_ref[h, rows, :]
            vb = jnp.concatenate([v_ref[2 * h, rows, :], v_ref[2 * h + 1, rows, :]], axis=1)
            decay = etot_ref[b:b + 1, h * GLA_DK:(h + 1) * GLA_DK]
            st_ref[b * GLA_HEADS + h] = st_ref[b * GLA_HEADS + h] * decay + lax.dot_general(
                vb.astype(jnp.bfloat16), kdb.astype(jnp.bfloat16),
                (((0,), (0,)), ((), ())), preferred_element_type=jnp.float32)
        pieces.append(proj(*filler[b]))
    for b in range(BATCH):
        rows = _batch_rows(b)
        for h in range(GLA_HEADS):
            o = lax.dot_general(q_ref[h, rows, :].astype(jnp.bfloat16),
                                st_ref[b * GLA_HEADS + h].astype(jnp.bfloat16),
                                (((1,), (1,)), ((), ())),
                                preferred_element_type=jnp.float32)
            o_ref[2 * h, rows, :] = o[:, :LANES]
            o_ref[2 * h + 1, rows, :] = o[:, LANES:]
        if b % 2 == 1:
            pieces.append(proj(*filler[BATCH + b // 2]))
    gr, gate_a, gate_b = (
        jnp.concatenate(pieces[i * N_GROUPS:(i + 1) * N_GROUPS], axis=1) for i in range(3))

    if not batch_major_in:
        hn_ref[1 - slot] = _rmsnorm(xn_ref[...], g_ref[...]).astype(jnp.bfloat16)

    log_lam = LRU_C * _log_sigmoid(lam_ref[...])
    hr_tiles = []
    for s in range(N_SLABS):
        lanes = slice(s * LANES, (s + 1) * LANES)
        r = jax.nn.sigmoid(gates[s][:, :LANES] + ba_ref[:, lanes])
        i = jax.nn.sigmoid(gates[s][:, LANES:] + bx_ref[:, lanes])
        log_a = r * log_lam[:, lanes]
        a = jnp.exp(log_a)
        u = _sqrt_nonneg(-jnp.tanh(log_a) * (a * a + 1.0)) * (i * xc[:, lanes])
        hcur = h_ref[:, lanes]
        hs = []
        for t in range(CHUNK):
            rws = slice(t * BATCH, (t + 1) * BATCH)
            hcur = a[rws] * hcur + u[rws]
            hs.append(hcur)
        h_ref[:, lanes] = hcur
        hr_tiles.append(jnp.concatenate(hs, axis=0))
    hr = jnp.concatenate(hr_tiles, axis=1)

    o_heads = []
    for h in range(GLA_HEADS):
        oh = jnp.concatenate([o_ref[2 * h], o_ref[2 * h + 1]], axis=1)
        o_heads.append(oh * lax.rsqrt(jnp.mean(oh * oh, axis=-1, keepdims=True) + EPS))
    o = jnp.concatenate(o_heads, axis=1)
    y_a = _bdot(o * gnorm_ref[...] * jax.nn.silu(g_out), wouta_ref[...])
    y_b = _bdot(hr * jax.nn.gelu(gr), woutb_ref[...])

    merged = (jax.nn.sigmoid(gate_a) * y_a + jax.nn.sigmoid(gate_b) * y_b).astype(jnp.bfloat16)
    for c in range(N_GROUPS):
        cols = _group(c)
        if batch_major_in:
            res = jnp.concatenate([xs_ref[2 * c], xs_ref[2 * c + 1]], axis=1)
        else:
            res = x_ref[:, cols]
        out_ref[:, cols] = res + _bdot(merged, wo_ref[:, cols])


def _ffn_kernel(x_ref, xn_ref, g_ref, wup_ref, cw_ref, cb_ref, wdown_ref, gfin_ref, out_ref,
                hn_ref, tail_ref, act_ref, *relayout_scratch, last):
    step = pl.program_id(0)
    slot = lax.rem(step, 2)

    @pl.when(step == 0)
    def _():
        tail_ref[...] = jnp.zeros_like(tail_ref)
        hn_ref[0] = _rmsnorm(x_ref[...], g_ref[...]).astype(jnp.bfloat16)

    hn_cur = hn_ref.at[slot]
    ctx = (FFN_CONV - 1) * BATCH
    n_tiles = FFN_DFF // FFN_TILE

    def up_tile(j):
        act = []
        for base in (j * FFN_TILE, FFN_DFF + j * FFN_TILE):
            cols = slice(base, base + FFN_TILE)
            up = _bdot(hn_cur[...], wup_ref[:, cols])
            upad = jnp.concatenate([tail_ref[:, cols], up], axis=0)
            tail_ref[:, cols] = up[ROWS - ctx:]
            act.append(_causal_conv(upad, cw_ref.at[:, cols], cb_ref.at[:, cols], FFN_CONV))
        val, gate = act
        return (jax.nn.gelu(gate) * val).astype(jnp.bfloat16)

    for j in range(n_tiles):
        act_ref[:, j * FFN_TILE:(j + 1) * FFN_TILE] = up_tile(j)
    hn_ref[1 - slot] = _rmsnorm(xn_ref[...], g_ref[...]).astype(jnp.bfloat16)
    if last:
        (xs_ref,) = relayout_scratch
        y = x_ref[...] + _bdot(act_ref[...], wdown_ref[...])
        _store_batch_major(_rmsnorm(y, gfin_ref[...]), xs_ref, out_ref)
    else:
        for c in range(N_GROUPS):
            cols = _group(c)
            out_ref[:, cols] = x_ref[:, cols] + _bdot(act_ref[...], wdown_ref[:, cols])


def _layer_block(arr, layer):
    shape = arr.shape[1:]
    zeros = (0,) * len(shape)
    return pl.BlockSpec((None,) + shape, lambda i: (layer,) + zeros,
                        pipeline_mode=pl.Buffered(1))


def _resident(arr):
    zeros = (0,) * arr.ndim
    return pl.BlockSpec(arr.shape, lambda i: zeros, pipeline_mode=pl.Buffered(1))


def _x_spec(batch_major, n_steps=None, lookahead=0):
    if batch_major:
        return pl.BlockSpec((BATCH, CHUNK, D_MODEL), lambda i: (0, i, 0))
    if lookahead:
        return pl.BlockSpec((ROWS, D_MODEL),
                            lambda i: (jnp.minimum(i + lookahead, n_steps - 1), 0))
    return pl.BlockSpec((ROWS, D_MODEL), lambda i: (i, 0))


def _relayout_scratch(needed):
    return [pltpu.VMEM((N_SLABS, ROWS, LANES), jnp.float32)] if needed else []


def _compiler_params():
    return pltpu.CompilerParams(dimension_semantics=("arbitrary",),
                                vmem_limit_bytes=VMEM_LIMIT_BYTES)


def _mixer_call(x, params, layer, batch_major_in):
    n_rows = x.size // D_MODEL
    n_steps = n_rows // ROWS
    slab = pltpu.VMEM((N_SLABS, ROWS, LANES), jnp.float32)
    head_slab = pltpu.VMEM((GLA_HEADS, ROWS, LANES), jnp.float32)
    x_args, x_specs = [x], [_x_spec(batch_major_in)]
    if not batch_major_in:
        x_args.append(x)
        x_specs.append(_x_spec(False, n_steps, lookahead=1))
    return pl.pallas_call(
        functools.partial(_mixer_kernel, batch_major_in=batch_major_in),
        grid=(n_steps,),
        in_specs=x_specs + [_layer_block(p, layer) for p in params],
        out_specs=_x_spec(False),
        out_shape=jax.ShapeDtypeStruct((n_rows, D_MODEL), x.dtype),
        scratch_shapes=[
            pltpu.VMEM((1 if batch_major_in else 2, ROWS, D_MODEL), jnp.bfloat16),
            head_slab, head_slab,
            pltpu.VMEM((BATCH, GLA_KW), jnp.float32),
            slab, slab,
            pltpu.VMEM((BATCH * GLA_HEADS, GLA_DV, GLA_DK), jnp.float32),
            pltpu.VMEM(((LRU_CONV - 1) * BATCH, D_MODEL), jnp.float32),
            pltpu.VMEM((BATCH, D_MODEL), jnp.float32),
        ] + _relayout_scratch(batch_major_in),
        compiler_params=_compiler_params(),
        name="token_mixer",
    )(*x_args, *params)


def _ffn_call(x, params, gfin, layer, last):
    n_rows = x.shape[0]
    n_steps = n_rows // ROWS
    out_shape = (BATCH, n_rows // BATCH, D_MODEL) if last else (n_rows, D_MODEL)
    return pl.pallas_call(
        functools.partial(_ffn_kernel, last=last),
        grid=(n_steps,),
        in_specs=([_x_spec(False), _x_spec(False, n_steps, lookahead=1)]
                  + [_layer_block(p, layer) for p in params] + [_resident(gfin)]),
        out_specs=_x_spec(last),
        out_shape=jax.ShapeDtypeStruct(out_shape, x.dtype),
        scratch_shapes=[
            pltpu.VMEM((2, ROWS, D_MODEL), jnp.bfloat16),
            pltpu.VMEM(((FFN_CONV - 1) * BATCH, 2 * FFN_DFF), jnp.float32),
            pltpu.VMEM((ROWS, FFN_DFF), jnp.bfloat16),
        ] + _relayout_scratch(last),
        compiler_params=_compiler_params(),
        name="channel_mixer",
    )(x, x, *params, gfin)


def _lru_gate_weights(w_a, w_x):
    def blockdiag(w):
        w = w.reshape(w.shape[0], N_SLABS, 2, LRU_BW, LRU_BW)
        z = jnp.zeros_like(w[:, :, 0])
        top = jnp.concatenate([w[:, :, 0], z], axis=3)
        bot = jnp.concatenate([z, w[:, :, 1]], axis=3)
        return jnp.concatenate([top, bot], axis=2)
    return jnp.concatenate([blockdiag(w_a), blockdiag(w_x)], axis=3).astype(jnp.bfloat16)


def _pack_w_in(w_in):
    points = [0]
    for width in IN_SPLITS:
        points.append(points[-1] + width)
    parts = []
    for name in PACKED_ORDER:
        i = PACKED_SOURCE[name]
        seg = w_in[:, :, points[i]:points[i + 1]]
        pad = PACKED_WIDTH[name] - seg.shape[2]
        parts.append(jnp.pad(seg, ((0, 0), (0, 0), (0, pad))) if pad else seg)
    return jnp.concatenate(parts, axis=2).astype(jnp.bfloat16)


def kernel(x, norm_mix, w_in, w_alpha, b_alpha, gla_norm, w_out_gla, lru_conv_w, lru_conv_b,
           lru_w_a, lru_b_a, lru_w_x, lru_b_x, lru_lambda, w_out_lru, w_o, norm_ffn, w_up,
           ffn_conv_w, ffn_conv_b, w_down, norm_final):
    depth = w_in.shape[0]
    bf = jnp.bfloat16
    rows = lambda a: a.reshape(a.shape[0], 1, -1)
    mixer_params = (
        rows(norm_mix), _pack_w_in(w_in),
        jnp.pad(w_alpha.astype(bf), ((0, 0), (0, LANES - GLA_RANK), (0, 0))), rows(b_alpha),
        rows(jnp.tile(gla_norm, (1, GLA_HEADS))), w_out_gla.astype(bf),
        lru_conv_w, rows(lru_conv_b), _lru_gate_weights(lru_w_a, lru_w_x),
        rows(lru_b_a), rows(lru_b_x), rows(lru_lambda), w_out_lru.astype(bf), w_o.astype(bf))
    ffn_params = (rows(norm_ffn), w_up.astype(bf), ffn_conv_w, rows(ffn_conv_b),
                  w_down.astype(bf))
    gfin = norm_final.reshape(1, -1)
    for l in range(depth):
        x = _mixer_call(x, mixer_params, l, batch_major_in=(l == 0))
        x = _ffn_call(x, ffn_params, gfin, l, last=(l == depth - 1))
    return x
```
